```python
import math
import jax, jax.numpy as jnp
from jax import lax
import numpy as np

D_MODEL = 2048
BATCH = 4
SEQ = 2048
DEPTH = 2

CHUNK = 64
Q_BLOCK = 128
N_BRANCH = 4
BRANCH_W = D_MODEL // 2
D_FF = ((8 * D_MODEL // 3 + 255) // 256) * 256
NORM_EPS = 1e-6

RW_HEAD_DIM = 64
RW_HEADS = BRANCH_W // RW_HEAD_DIM
RW_R_W = max(32, int(round(1.8 * D_MODEL ** 0.5 / 32)) * 32)
RW_R_A = max(32, int(round(1.8 * D_MODEL ** 0.5 / 32)) * 32)
RW_R_G = max(32, int(round(0.6 * D_MODEL ** 0.8 / 32)) * 32)
RW_GN_EPS = 64e-5
RW_COLS = 3 * BRANCH_W + RW_R_W + RW_R_A + RW_R_G

SB_HEAD_DIM = 128
SB_HEADS = BRANCH_W // SB_HEAD_DIM
SB_COLS = 3 * BRANCH_W

FX_HEAD_DIM = 128
FX_HEADS = BRANCH_W // FX_HEAD_DIM
FX_COLS = 3 * BRANCH_W + FX_HEADS

DN_HEAD_DIM = 128
DN_HEADS = BRANCH_W // DN_HEAD_DIM
DN_CONV = 4
DN_COLS = 4 * BRANCH_W + 2 * DN_HEADS

N_IN = RW_COLS + SB_COLS + FX_COLS + DN_COLS

kernel_name = "hybrid_streaming_encoder_block"


def _split(t, sizes):
    offs = np.cumsum((0,) + tuple(sizes))
    return [t[..., int(a):int(b)] for a, b in zip(offs[:-1], offs[1:])]


def rms_norm(x, g, eps=NORM_EPS):
    xf = x.astype(jnp.float32)
    y = xf * lax.rsqrt(jnp.mean(xf * xf, -1, keepdims=True) + eps)
    return (y * g.astype(jnp.float32)).astype(x.dtype)


def _l2norm(t):
    tf = t.astype(jnp.float32)
    return tf * lax.rsqrt(jnp.sum(tf * tf, -1, keepdims=True) + 1e-12)


def swiglu(h, w_in, w_out):
    gate, up = jnp.split(h @ w_in, 2, axis=-1)
    return (jax.nn.silu(gate) * up) @ w_out


def _rwkv7_scan(r, w, k, v, a, b):
    Bsz, T, H, N = r.shape

    def step(S, inp):
        r_t, w_t, k_t, v_t, a_t, b_t = inp
        Sa = jnp.einsum('bhvk,bhk->bhv', S, a_t)
        S = S * w_t[:, :, None, :] + Sa[..., None] * b_t[:, :, None, :] + v_t[..., None] * k_t[:, :, None, :]
        return S, jnp.einsum('bhvk,bhk->bhv', S, r_t)

    xs = tuple(jnp.moveaxis(t, 1, 0) for t in (r, w, k, v, a, b))
    _, o = lax.scan(step, jnp.zeros((Bsz, H, N, N), jnp.float32), xs)
    return jnp.moveaxis(o, 0, 1)


def rwkv7_mixer(p, mu, vec, r_k, w2, a2, g2):
    dtype = p.dtype
    p = p.astype(jnp.float32)
    Bsz, T, _ = p.shape
    p_prev = jnp.pad(p, ((0, 0), (1, 0), (0, 0)))[:, :T]
    p = p + (p_prev - p) * mu
    r, k, v, w_lo, a_lo, g_lo = _split(p, (BRANCH_W, BRANCH_W, BRANCH_W, RW_R_W, RW_R_A, RW_R_G))
    w0, a0, k_k, k_a, ln_w, ln_b = vec
    log_w = -jnp.exp(-jax.nn.softplus(-(w0 + jnp.tanh(w_lo) @ w2)) - 0.5)
    a = jax.nn.sigmoid(a0 + a_lo @ a2)
    g = jax.nn.sigmoid(g_lo) @ g2
    heads = lambda t: t.reshape(Bsz, T, RW_HEADS, RW_HEAD_DIM)
    kk = _l2norm(heads(k * k_k))
    k = k * (1.0 + (a - 1.0) * k_a)
    r_h, k_h, v_h, a_h = heads(r), heads(k), heads(v), heads(a)
    o = _rwkv7_scan(r_h, jnp.exp(heads(log_w)), k_h, v_h, -kk, kk * a_h)
    mean = jnp.mean(o, -1, keepdims=True)
    var = jnp.mean(jnp.square(o - mean), -1, keepdims=True)
    o = ((o - mean) * lax.rsqrt(var + RW_GN_EPS)).reshape(Bsz, T, BRANCH_W) * ln_w + ln_b
    bonus = jnp.sum(r_h * k_h * r_k, -1, keepdims=True) * v_h
    return ((o + bonus.reshape(Bsz, T, BRANCH_W)) * g).astype(dtype)


def stick_breaking_mixer(p):
    Bsz, T, _ = p.shape
    q, k, v = [t.reshape(Bsz, T, SB_HEADS, SB_HEAD_DIM) for t in _split(p, (BRANCH_W,) * 3)]
    scale = SB_HEAD_DIM ** -0.5
    outs = []
    for start in range(0, T, Q_BLOCK):
        end = start + Q_BLOCK
        z = jnp.einsum('bqhd,bkhd->bhqk', q[:, start:end], k[:, :end]).astype(jnp.float32) * scale
        t_idx = start + jnp.arange(Q_BLOCK)[:, None]
        s_idx = jnp.arange(end)[None, :]
        strict = s_idx < t_idx
        log_keep = jnp.where(strict, jax.nn.log_sigmoid(-z), 0.0)
        later = lax.cumsum(log_keep, axis=3, reverse=True) - log_keep
        A = jnp.where(strict, jnp.exp(jax.nn.log_sigmoid(z) + later), 0.0)
        outs.append(jnp.einsum('bhqk,bkhd->bqhd', A.astype(v.dtype), v[:, :end]))
    return jnp.concatenate(outs, axis=1).reshape(Bsz, T, BRANCH_W)


def fox_mixer(p, qk_g, b_f):
    Bsz, T, _ = p.shape
    q, k, v, f_logit = _split(p, (BRANCH_W, BRANCH_W, BRANCH_W, FX_HEADS))
    q = rms_norm(q.reshape(Bsz, T, FX_HEADS, FX_HEAD_DIM), qk_g[0])
    k = rms_norm(k.reshape(Bsz, T, FX_HEADS, FX_HEAD_DIM), qk_g[1])
    v = v.reshape(Bsz, T, FX_HEADS, FX_HEAD_DIM)
    log_f = jax.nn.log_sigmoid((f_logit + b_f).astype(jnp.float32))
    cum = jnp.cumsum(log_f, axis=1).transpose(0, 2, 1)
    scale = FX_HEAD_DIM ** -0.5
    outs = []
    for start in range(0, T, Q_BLOCK):
        end = start + Q_BLOCK
        logits = jnp.einsum('bqhd,bkhd->bhqk', q[:, start:end], k[:, :end]).astype(jnp.float32) * scale
        logits = logits + cum[:, :, start:end, None] - cum[:, :, None, :end]
        causal = jnp.arange(end)[None, :] <= (start + jnp.arange(Q_BLOCK)[:, None])
        w = jax.nn.softmax(jnp.where(causal, logits, -jnp.inf), axis=-1)
        outs.append(jnp.einsum('bhqk,bkhd->bqhd', w.astype(v.dtype), v[:, :end]))
    return jnp.concatenate(outs, axis=1).reshape(Bsz, T, BRANCH_W)


def causal_depthwise_conv(x, w):
    K, C = w.shape
    return lax.conv_general_dilated(x, w[:, None, :].astype(x.dtype), window_strides=(1,),
                                    padding=[(K - 1, 0)], dimension_numbers=('NWC', 'WIO', 'NWC'),
                                    feature_group_count=C)


def chunk_gated_delta_rule(q, k, v, g, beta):
    Bsz, T, H, Dk = q.shape
    Dv = v.shape[-1]
    n = T // CHUNK

    def chunks(t):
        t = t.reshape((Bsz, n, CHUNK, H) + t.shape[3:])
        return jnp.moveaxis(t, (1, 3), (0, 2))

    qc, kc, vc, bc = chunks(q), chunks(k), chunks(v), chunks(beta)
    gc = jnp.cumsum(chunks(g), axis=-1)
    incl = jnp.tril(jnp.ones((CHUNK, CHUNK), bool))
    strict = jnp.tril(jnp.ones((CHUNK, CHUNK), bool), -1)
    decay = jnp.exp(jnp.where(incl, gc[..., :, None] - gc[..., None, :], -jnp.inf))
    kb = kc * bc[..., None]
    L = jnp.where(strict, jnp.einsum('nbhid,nbhjd->nbhij', kb, kc) * decay, 0.0)
    rhs = jnp.concatenate([vc * bc[..., None], kb * jnp.exp(gc)[..., None]], axis=-1)
    uw = lax.linalg.triangular_solve(L + jnp.eye(CHUNK, dtype=L.dtype), rhs, left_side=True, lower=True)
    u, w = uw[..., :Dv], uw[..., Dv:]
    attn = jnp.where(incl, jnp.einsum('nbhid,nbhjd->nbhij', qc, kc) * decay, 0.0)

    def step(S, inp):
        q_i, k_i, u_i, w_i, g_i, attn_i = inp
        v_new = u_i - jnp.einsum('bhck,bhkv->bhcv', w_i, S)
        o_i = (jnp.einsum('bhck,bhkv->bhcv', q_i * jnp.exp(g_i)[..., None], S)
               + jnp.einsum('bhij,bhjv->bhiv', attn_i, v_new))
        g_last = g_i[..., -1:]
        S = S * jnp.exp(g_last)[..., None] + jnp.einsum(
            'bhck,bhcv->bhkv', k_i * jnp.exp(g_last - g_i)[..., None], v_new)
        return S, o_i

    S0 = jnp.zeros((Bsz, H, Dk, Dv), jnp.float32)
    _, o = lax.scan(step, S0, (qc, kc, u, w, gc, attn))
    return jnp.moveaxis(o, (0, 2), (1, 3)).reshape(Bsz, T, H, Dv)


def gated_deltanet_mixer(p, conv_w, A_log, dt_bias, norm_g):
    Bsz, T, _ = p.shape
    qkv, z, beta_logit, a_logit = _split(p, (3 * BRANCH_W, BRANCH_W, DN_HEADS, DN_HEADS))
    qkv = jax.nn.silu(causal_depthwise_conv(qkv, conv_w))
    q, k, v = [t.reshape(Bsz, T, DN_HEADS, DN_HEAD_DIM) for t in _split(qkv, (BRANCH_W,) * 3)]
    q = _l2norm(q) * DN_HEAD_DIM ** -0.5
    k = _l2norm(k)
    beta = jax.nn.sigmoid(beta_logit.astype(jnp.float32))
    g = -jnp.exp(A_log.astype(jnp.float32)) * jax.nn.softplus((a_logit + dt_bias).astype(jnp.float32))
    o = chunk_gated_delta_rule(q, k, v.astype(jnp.float32), g, beta)
    o = rms_norm(o, norm_g) * jax.nn.silu(z.reshape(Bsz, T, DN_HEADS, DN_HEAD_DIM).astype(jnp.float32))
    return o.reshape(Bsz, T, BRANCH_W).astype(p.dtype)


def hybrid_mixer(h, w_in, w_gate, b_gate, w_branch, w_out, rwkv_mu, rwkv_vec, rwkv_r_k, rwkv_w2,
                 rwkv_a2, rwkv_g2, fox_qk_g, fox_b_f, dn_conv, dn_A_log, dn_dt_bias, dn_norm_g):
    p = h @ w_in
    pA, pB, pC, pD = _split(p, (RW_COLS, SB_COLS, FX_COLS, DN_COLS))
    ys = (rwkv7_mixer(pA, rwkv_mu, rwkv_vec, rwkv_r_k, rwkv_w2, rwkv_a2, rwkv_g2),
          stick_breaking_mixer(pB),
          fox_mixer(pC, fox_qk_g, fox_b_f),
          gated_deltanet_mixer(pD, dn_conv, dn_A_log, dn_dt_bias, dn_norm_g))
    merged = None
    for i in range(N_BRANCH):
        gate = jax.nn.sigmoid(h @ w_gate[i] + b_gate[i])
        term = gate * (ys[i] @ w_branch[i])
        merged = term if merged is None else merged + term
    return merged @ w_out


def setup_inputs(seed: int = 0) -> dict:
    key = jax.random.key(seed)
    ks = iter(jax.random.split(key, 32))
    nrm = lambda shape, s: s * jax.random.normal(next(ks), shape, jnp.float32)
    unif = lambda shape, lo, hi: jax.random.uniform(next(ks), shape, jnp.float32, lo, hi)
    L, D = DEPTH, D_MODEL
    x = nrm((BATCH, SEQ, D), 1.0)
    norm_g = 1.0 + nrm((L, 6, D), 0.05)
    ffn_w_in = nrm((L, 2, D, 2 * D_FF), D ** -0.5)
    ffn_w_out = nrm((L, 2, D_FF, D), D_FF ** -0.5)
    w_in = nrm((L, D, N_IN), D ** -0.5)
    w_gate = nrm((L, N_BRANCH, D, D), D ** -0.5)
    b_gate = nrm((L, N_BRANCH, D), 0.1)
    w_branch = nrm((L, N_BRANCH, BRANCH_W, D), BRANCH_W ** -0.5)
    w_out = nrm((L, D, D), D ** -0.5)
    rwkv_mu = unif((L, RW_COLS), 0.0, 1.0)
    rwkv_vec = jnp.stack([unif((L, BRANCH_W), -6.0, -1.0),
                          nrm((L, BRANCH_W), 0.1),
                          0.85 + nrm((L, BRANCH_W), 0.05),
                          1.0 + nrm((L, BRANCH_W), 0.05),
                          1.0 + nrm((L, BRANCH_W), 0.05),
                          nrm((L, BRANCH_W), 0.02)], axis=1)
    rwkv_r_k = nrm((L, RW_HEADS, RW_HEAD_DIM), 0.1)
    rwkv_w2 = nrm((L, RW_R_W, BRANCH_W), 0.1)
    rwkv_a2 = nrm((L, RW_R_A, BRANCH_W), 0.1)
    rwkv_g2 = nrm((L, RW_R_G, BRANCH_W), RW_R_G ** -0.5)
    fox_qk_g = 1.0 + nrm((L, 2, FX_HEAD_DIM), 0.05)
    fox_b_f = 1.0 + nrm((L, FX_HEADS), 0.5)
    dn_conv = nrm((L, DN_CONV, 3 * BRANCH_W), DN_CONV ** -0.5)
    dn_A_log = jnp.log(unif((L, DN_HEADS), 1.0, 16.0))
    dt = jnp.exp(unif((L, DN_HEADS), math.log(1e-3), math.log(1e-1)))
    dn_dt_bias = dt + jnp.log(-jnp.expm1(-dt))
    dn_norm_g = 1.0 + nrm((L, DN_HEAD_DIM), 0.05)
    return {"x": x, "norm_g": norm_g, "ffn_w_in": ffn_w_in, "ffn_w_out": ffn_w_out, "w_in": w_in,
            "w_gate": w_gate, "b_gate": b_gate, "w_branch": w_branch, "w_out": w_out,
            "rwkv_mu": rwkv_mu, "rwkv_vec": rwkv_vec, "rwkv_r_k": rwkv_r_k, "rwkv_w2": rwkv_w2,
            "rwkv_a2": rwkv_a2, "rwkv_g2": rwkv_g2, "fox_qk_g": fox_qk_g, "fox_b_f": fox_b_f,
            "dn_conv": dn_conv, "dn_A_log": dn_A_log, "dn_dt_bias": dn_dt_bias, "dn_norm_g": dn_norm_g}


def reference(x, norm_g, ffn_w_in, ffn_w_out, w_in, w_gate, b_gate, w_branch, w_out, rwkv_mu,
              rwkv_vec, rwkv_r_k, rwkv_w2, rwkv_a2, rwkv_g2, fox_qk_g, fox_b_f, dn_conv, dn_A_log,
              dn_dt_bias, dn_norm_g):
    for l in range(DEPTH):
        g = norm_g[l]
        h = swiglu(rms_norm(x, g[0]), ffn_w_in[l, 0], ffn_w_out[l, 0])
        x = x + 0.5 * rms_norm(h, g[1])
        h = rms_norm(x, g[2])
        m = hybrid_mixer(h, w_in[l], w_gate[l], b_gate[l], w_branch[l], w_out[l], rwkv_mu[l],
                         rwkv_vec[l], rwkv_r_k[l], rwkv_w2[l], rwkv_a2[l], rwkv_g2[l], fox_qk_g[l],
                         fox_b_f[l], dn_conv[l], dn_A_log[l], dn_dt_bias[l], dn_norm_g[l])
        x = x + rms_norm(m, g[3])
        h = swiglu(rms_norm(x, g[4]), ffn_w_in[l, 1], ffn_w_out[l, 1])
        x = x + 0.5 * rms_norm(h, g[5])
    return x
```

```python
import functools

import jax
import jax.numpy as jnp
from jax import lax
from jax.experimental import pallas as pl
from jax.experimental.pallas import tpu as pltpu

F32 = jnp.float32
BF16 = jnp.bfloat16

NORM_EPS = 1e-6
RW_GN_EPS = 64e-5
L2_EPS = 1e-12

LANES = 128
SUBLANES = 8
VMEM_LIMIT = 56 * 1024 * 1024

RW_HD = 64
RW_CHUNK = 64
HD = 128
DN_CHUNK = 128
DN_CONV = 4
ATT_BLK = 128


def _cparams(sem):
    return pltpu.CompilerParams(dimension_semantics=sem, vmem_limit_bytes=VMEM_LIMIT)


def _dot(a, b):
    return jnp.dot(a, b, preferred_element_type=F32)


def _dot_nt(a, b):
    return lax.dot_general(a, b, (((1,), (1,)), ((), ())), preferred_element_type=F32)


def _dot_tn(a, b):
    return lax.dot_general(a, b, (((0,), (0,)), ((), ())), preferred_element_type=F32)


def _split(x):
    hi = x.astype(BF16)
    lo = (x - hi.astype(F32)).astype(BF16)
    return hi, lo


def _dot_x2(a_exact, x):
    hi, lo = _split(x)
    return _dot(a_exact, hi) + _dot(a_exact, lo)


def _dot_2x(x, b_exact):
    hi, lo = _split(x)
    return _dot(hi, b_exact) + _dot(lo, b_exact)


def _dot_3(a, b):
    ah, al = _split(a)
    bh, bl = _split(b)
    return _dot(ah, bh) + _dot(ah, bl) + _dot(al, bh)


def _sigmoid(x):
    return 1.0 / (1.0 + jnp.exp(-x))


def _softplus(x):
    return jnp.maximum(x, 0.0) + jnp.log(1.0 + jnp.exp(-jnp.abs(x)))


def _rms(x, g):
    return x * lax.rsqrt(jnp.mean(x * x, axis=-1, keepdims=True) + NORM_EPS) * g


def _iota2(shape, dim):
    return lax.broadcasted_iota(jnp.int32, shape, dim)


def _tri(n, strict):
    r, c = _iota2((n, n), 0), _iota2((n, n), 1)
    return (c < r) if strict else (c <= r)


def _shift_rows(x, halo, k):
    rolled = pltpu.roll(x, k, axis=0)
    hx = pltpu.roll(halo, k, axis=0)
    top = jnp.where(_iota2(hx.shape, 0) < k, hx, rolled[0:SUBLANES])
    return jnp.concatenate([top, rolled[SUBLANES:]], axis=0)


def _inv_unit_lower(n_mat, steps):
    n = n_mat.shape[0]
    eye = jnp.where(_iota2((n, n), 0) == _iota2((n, n), 1), 1.0, 0.0)
    p = eye + n_mat
    m = n_mat
    for _ in range(steps):
        mb = m.astype(BF16)
        m = _dot(mb, mb)
        p = p + _dot(p.astype(BF16), m.astype(BF16))
    res = (eye - p) + _dot_3(n_mat, p)
    return p + _dot(p.astype(BF16), res.astype(BF16))


def _ffn_kernel(x_ref, g0_ref, g1_ref, wg_ref, wu_ref, wo_ref, o_ref, xn_ref, acc_ref):
    j = pl.program_id(1)

    @pl.when(j == 0)
    def _():
        xn_ref[...] = _rms(x_ref[...], g0_ref[...]).astype(BF16)
        acc_ref[...] = jnp.zeros_like(acc_ref)

    xn = xn_ref[...]
    gate = _dot(xn, wg_ref[...])
    up = _dot(xn, wu_ref[...])
    act = (gate * _sigmoid(gate) * up).astype(BF16)
    acc_ref[...] += _dot(act, wo_ref[...])

    @pl.when(j == pl.num_programs(1) - 1)
    def _():
        o_ref[...] = x_ref[...] + 0.5 * _rms(acc_ref[...], g1_ref[...])


def _ffn(x, g0, g1, w_in, w_out, tm=512, tf=512):
    m, d = x.shape
    dff = w_out.shape[0]
    nf = dff // tf
    return pl.pallas_call(
        _ffn_kernel,
        out_shape=jax.ShapeDtypeStruct((m, d), F32),
        grid=(m // tm, nf),
        in_specs=[
            pl.BlockSpec((tm, d), lambda i, j: (i, 0)),
            pl.BlockSpec((1, d), lambda i, j: (0, 0)),
            pl.BlockSpec((1, d), lambda i, j: (0, 0)),
            pl.BlockSpec((d, tf), lambda i, j: (0, j)),
            pl.BlockSpec((d, tf), lambda i, j: (0, j + nf)),
            pl.BlockSpec((tf, d), lambda i, j: (j, 0)),
        ],
        out_specs=pl.BlockSpec((tm, d), lambda i, j: (i, 0)),
        scratch_shapes=[pltpu.VMEM((tm, d), BF16), pltpu.VMEM((tm, d), F32)],
        compiler_params=_cparams(("parallel", "arbitrary")),
        name="ffn",
    )(x, g0, g1, w_in, w_in, w_out)


def _normproj_kernel(x_ref, g_ref, w_ref, o_ref, xn_ref):
    @pl.when(pl.program_id(1) == 0)
    def _():
        xn_ref[...] = _rms(x_ref[...], g_ref[...]).astype(BF16)

    o_ref[...] = _dot(xn_ref[...], w_ref[...])


def _normproj(x, g, w, tn, tm=512, name="normproj"):
    m, d = x.shape
    n = w.shape[1]
    return pl.pallas_call(
        _normproj_kernel,
        out_shape=[jax.ShapeDtypeStruct((m, n), F32), jax.ShapeDtypeStruct((m, d), BF16)],
        grid=(m // tm, n // tn),
        in_specs=[
            pl.BlockSpec((tm, d), lambda i, j: (i, 0)),
            pl.BlockSpec((1, d), lambda i, j: (0, 0)),
            pl.BlockSpec((d, tn), lambda i, j: (0, j)),
        ],
        out_specs=[pl.BlockSpec((tm, tn), lambda i, j: (i, j)), pl.BlockSpec((tm, d), lambda i, j: (i, 0))],
        compiler_params=_cparams(("parallel", "arbitrary")),
        name=name,
    )(x, g, w)


RW_W = 1024
RW_PA = 3584


def _rwkv_pre_kernel(seq_len, p_ref, halo_ref, mu_ref, vec_ref, rk_ref, w2_ref, a2_ref, g2_ref, e_ref,
                     r_ref, lw_ref, k_ref, v_ref, nkk_ref, b_ref, g_ref, bonus_ref):
    tr = p_ref.shape[0]
    x = p_ref[...]
    first = (pl.program_id(0) * tr) % seq_len == 0
    halo = jnp.where(first, 0.0, halo_ref[...])
    prev = _shift_rows(x, halo, 1)
    x = x + (prev - x) * mu_ref[...]
    w = RW_W
    r, k, v = x[:, 0:w], x[:, w:2 * w], x[:, 2 * w:3 * w]
    w_lo, a_lo, g_lo = x[:, 3 * w:3 * w + 128], x[:, 3 * w + 128:3 * w + 256], x[:, 3 * w + 256:3 * w + 512]
    w0, a0, k_k, k_a = vec_ref[0:1, :], vec_ref[1:2, :], vec_ref[2:3, :], vec_ref[3:4, :]
    lw = -jnp.exp(-0.5) * _sigmoid(w0 + _dot_3(jnp.tanh(w_lo), w2_ref[...]))
    a = _sigmoid(a0 + _dot_3(a_lo, a2_ref[...]))
    g = _dot_3(_sigmoid(g_lo), g2_ref[...])
    e = e_ref[...]
    kk = k * k_k
    kk = kk * lax.rsqrt(_dot_2x(kk * kk, e) + L2_EPS)
    k = k * (1.0 + (a - 1.0) * k_a)
    bonus = _dot_2x(r * k * rk_ref[...], e) * v
    r_ref[...] = r
    lw_ref[...] = lw
    k_ref[...] = k
    v_ref[...] = v
    nkk_ref[...] = -kk
    b_ref[...] = kk * a
    g_ref[...] = g
    bonus_ref[...] = bonus


def _rwkv_scan_kernel(r_ref, lw_ref, k_ref, v_ref, nkk_ref, b_ref, o_ref, s_ref):
    @pl.when(pl.program_id(1) == 0)
    def _():
        s_ref[...] = jnp.zeros_like(s_ref)

    c = RW_CHUNK
    lw = lw_ref[...]
    ltri = jnp.where(_tri(c, strict=False), 1.0, 0.0).astype(BF16)
    cum = _dot_x2(ltri, lw)
    tot = cum[c - 1:c, :]
    e_in, e_neg = jnp.exp(cum), jnp.exp(-cum)
    e_ex, e_end = jnp.exp(cum - lw), jnp.exp(tot - cum)
    r, k, v, nkk, b = r_ref[...], k_ref[...], v_ref[...], nkk_ref[...], b_ref[...]
    rt, kt, bt, at = r * e_in, k * e_neg, b * e_neg, nkk * e_ex
    kh, bh = k * e_end, b * e_end
    e_tot = jnp.exp(tot)

    n2 = 2 * c
    lane_lo = _iota2((c, LANES), 1) < RW_HD
    ri, ci = _iota2((n2, n2), 0) & (c - 1), _iota2((n2, n2), 1) & (c - 1)
    strict, incl = ci < ri, ci <= ri

    def stack(xp):
        return jnp.concatenate([jnp.where(lane_lo, xp, 0.0), jnp.where(lane_lo, 0.0, xp)], axis=0).astype(BF16)

    for pr in range(RW_W // LANES):
        sl = slice(pr * LANES, (pr + 1) * LANES)
        rt_s, kt_s, bt_s, at_s = stack(rt[:, sl]), stack(kt[:, sl]), stack(bt[:, sl]), stack(at[:, sl])
        kh_s, bh_s, v_s = stack(kh[:, sl]), stack(bh[:, sl]), stack(v[:, sl])
        gram = _dot_nt(jnp.concatenate([at_s, rt_s], axis=0), jnp.concatenate([bt_s, kt_s], axis=0))
        a_ab = jnp.where(strict, gram[:n2, :n2], 0.0)
        a_ak = jnp.where(strict, gram[:n2, n2:], 0.0)
        a_rb = jnp.where(incl, gram[n2:, :n2], 0.0)
        a_rk = jnp.where(incl, gram[n2:, n2:], 0.0)
        t_inv = _inv_unit_lower(a_ab, 5).astype(BF16)
        akv = _dot(a_ak.astype(BF16), v_s)
        wu = _dot(t_inv, jnp.concatenate([at_s, akv.astype(BF16)], axis=1))
        s0 = s_ref[pr]
        s0b = s0.astype(BF16)
        ws = _dot_nt(jnp.concatenate([wu[:, :LANES].astype(BF16), rt_s], axis=0), s0b)
        u = ws[:n2] + wu[:, LANES:]
        uv = jnp.concatenate([u.astype(BF16), v_s], axis=0)
        o_s = ws[n2:] + _dot(jnp.concatenate([a_rb, a_rk], axis=1).astype(BF16), uv)
        s_ref[pr] = s0 * e_tot[:, sl] + _dot_tn(uv, jnp.concatenate([bh_s, kh_s], axis=0))
        o_ref[:, sl] = o_s[:c] + o_s[c:]


def _rwkv_post_kernel(o_ref, bonus_ref, g_ref, ln_ref, e_ref, y_ref):
    o = o_ref[...]
    e = e_ref[...]
    mean = _dot_2x(o, e) * (1.0 / RW_HD)
    d = o - mean
    var = _dot_2x(d * d, e) * (1.0 / RW_HD)
    o = d * lax.rsqrt(var + RW_GN_EPS) * ln_ref[0:1, :] + ln_ref[1:2, :]
    y_ref[...] = ((o + bonus_ref[...]) * g_ref[...]).astype(BF16)


def _rwkv_mixer(pa, batch, seq_len, mu, vec, r_k, w2, a2, g2, e_blk, tr=256):
    m = pa.shape[0]
    w = RW_W
    row = lambda i: (i, 0)
    fix = lambda i: (0, 0)
    wide = jax.ShapeDtypeStruct((m, w), F32)
    r, lw, k, v, nkk, b, g, bonus = pl.pallas_call(
        functools.partial(_rwkv_pre_kernel, seq_len),
        out_shape=[wide] * 8,
        grid=(m // tr,),
        in_specs=[
            pl.BlockSpec((tr, RW_PA), row),
            pl.BlockSpec((SUBLANES, RW_PA), lambda i: (jnp.maximum(i * (tr // SUBLANES) - 1, 0), 0)),
            pl.BlockSpec((1, RW_PA), fix),
            pl.BlockSpec((4, w), fix),
            pl.BlockSpec((1, w), fix),
            pl.BlockSpec((LANES, w), fix),
            pl.BlockSpec((LANES, w), fix),
            pl.BlockSpec((2 * LANES, w), fix),
            pl.BlockSpec((w, w), fix),
        ],
        out_specs=[pl.BlockSpec((tr, w), row)] * 8,
        compiler_params=_cparams(("parallel",)),
        name="rwkv_pre",
    )(pa, pa, mu, vec[0:4], r_k, w2, a2, g2, e_blk)

    c = RW_CHUNK
    nchunk = seq_len // c
    blk = pl.BlockSpec((c, w), lambda bi, ci: (bi * nchunk + ci, 0))
    o = pl.pallas_call(
        _rwkv_scan_kernel,
        out_shape=wide,
        grid=(batch, nchunk),
        in_specs=[blk] * 6,
        out_specs=blk,
        scratch_shapes=[pltpu.VMEM((w // LANES, LANES, LANES), F32)],
        compiler_params=_cparams(("parallel", "arbitrary")),
        name="rwkv_scan",
    )(r, lw, k, v, nkk, b)

    return pl.pallas_call(
        _rwkv_post_kernel,
        out_shape=jax.ShapeDtypeStruct((m, w), BF16),
        grid=(m // tr,),
        in_specs=[pl.BlockSpec((tr, w), row)] * 3 + [pl.BlockSpec((2, w), fix), pl.BlockSpec((w, w), fix)],
        out_specs=pl.BlockSpec((tr, w), row),
        compiler_params=_cparams(("parallel",)),
        name="rwkv_post",
    )(o, bonus, g, vec[4:6], e_blk)


def _sb_kernel(q_ref, k_ref, v_ref, o_ref):
    i = pl.program_id(2)
    tb = ATT_BLK
    q = (q_ref[...] * (HD ** -0.5)).astype(BF16)
    r, c = _iota2((tb, tb), 0), _iota2((tb, tb), 1)
    upper = jnp.where(r > c, 1.0, 0.0).astype(BF16)
    diag_ok = c < r

    def block(j, carry, acc, masked):
        kj = k_ref[pl.ds(j * tb, tb), :].astype(BF16)
        vj = v_ref[pl.ds(j * tb, tb), :].astype(BF16)
        z = _dot_nt(q, kj)
        sp = _softplus(z)
        lk = -sp
        if masked:
            lk = jnp.where(diag_ok, lk, 0.0)
        later = _dot_2x(lk, upper) + carry
        a = jnp.exp(z - sp + later)
        if masked:
            a = jnp.where(diag_ok, a, 0.0)
        acc = acc + _dot(a.astype(BF16), vj)
        carry = later[:, 0:1] + lk[:, 0:1]
        return carry, acc

    carry, acc = block(i, jnp.zeros((tb, 1), F32), jnp.zeros((tb, HD), F32), True)

    def body(n, st):
        return block(i - 1 - n, st[0], st[1], False)

    carry, acc = lax.fori_loop(0, i, body, (carry, acc))
    o_ref[...] = acc.astype(BF16)


def _sb_mixer(pb, batch, seq_len):
    m = pb.shape[0]
    tb = ATT_BLK
    nq = seq_len // tb
    nh = 1024 // HD
    return pl.pallas_call(
        _sb_kernel,
        out_shape=jax.ShapeDtypeStruct((m, 1024), BF16),
        grid=(batch, nh, nq),
        in_specs=[
            pl.BlockSpec((tb, HD), lambda b, h, i: (b * nq + i, h)),
            pl.BlockSpec((seq_len, HD), lambda b, h, i: (b, nh + h)),
            pl.BlockSpec((seq_len, HD), lambda b, h, i: (b, 2 * nh + h)),
        ],
        out_specs=pl.BlockSpec((tb, HD), lambda b, h, i: (b * nq + i, h)),
        compiler_params=_cparams(("parallel", "parallel", "arbitrary")),
        name="stick_breaking",
    )(pb, pb, pb)


def _fox_cum_kernel(f_ref, bf_ref, cum_ref, cumt_ref):
    tb = LANES
    x = f_ref[...] + bf_ref[...]
    lf = jnp.minimum(x, 0.0) - jnp.log(1.0 + jnp.exp(-jnp.abs(x)))
    ltri = jnp.where(_tri(tb, strict=False), 1.0, 0.0).astype(BF16)
    carry = jnp.zeros((1, LANES), F32)
    for blk in range(f_ref.shape[0] // tb):
        xb = lf[blk * tb:(blk + 1) * tb, :]
        hi = xb.astype(BF16)
        mid = (xb - hi.astype(F32)).astype(BF16)
        lo = (xb - hi.astype(F32) - mid.astype(F32)).astype(BF16)
        cb = _dot(ltri, hi) + _dot(ltri, mid) + _dot(ltri, lo) + carry
        cum_ref[blk * tb:(blk + 1) * tb, :] = cb
        cumt_ref[:, blk * tb:(blk + 1) * tb] = cb.T
        carry = cb[tb - 1:tb, :]


def _fox_kernel(q_ref, k_ref, v_ref, g_ref, cq_ref, ck_ref, o_ref, kn_ref):
    h = pl.program_id(1)
    i = pl.program_id(2)
    tb = ATT_BLK

    def hnorm(x, g):
        return x * lax.rsqrt(jnp.mean(x * x, axis=-1, keepdims=True) + NORM_EPS) * g

    @pl.when(i == 0)
    def _():
        kn_ref[...] = hnorm(k_ref[...], g_ref[1:2, :]).astype(BF16)

    q = (hnorm(q_ref[...], g_ref[0:1, :]) * (HD ** -0.5)).astype(BF16)
    cq = jnp.sum(jnp.where(_iota2((tb, LANES), 1) == h, cq_ref[...], 0.0), axis=-1, keepdims=True)
    r, c = _iota2((tb, tb), 0), _iota2((tb, tb), 1)
    causal = c <= r

    def block(j, m_run, l_run, acc, masked):
        kj = kn_ref[pl.ds(j * tb, tb), :]
        vj = v_ref[pl.ds(j * tb, tb), :].astype(BF16)
        ck8 = ck_ref[:, pl.ds(pl.multiple_of(j * tb, tb), tb)]
        ck = jnp.sum(jnp.where(_iota2(ck8.shape, 0) == h, ck8, 0.0), axis=0, keepdims=True)
        s = _dot_nt(q, kj) + cq - ck
        if masked:
            s = jnp.where(causal, s, -jnp.inf)
        m_new = jnp.maximum(m_run, jnp.max(s, axis=-1, keepdims=True))
        alpha = jnp.exp(m_run - m_new)
        p = jnp.exp(s - m_new)
        l_new = alpha * l_run + jnp.sum(p, axis=-1, keepdims=True)
        acc = alpha * acc + _dot(p.astype(BF16), vj)
        return m_new, l_new, acc

    st = block(i, jnp.full((tb, 1), -jnp.inf, F32), jnp.zeros((tb, 1), F32), jnp.zeros((tb, HD), F32), True)

    def body(n, st):
        return block(i - 1 - n, st[0], st[1], st[2], False)

    m_run, l_run, acc = lax.fori_loop(0, i, body, st)
    o_ref[...] = (acc / l_run).astype(BF16)


def _fox_mixer(pc, batch, seq_len, qk_g, b_f_pad):
    m = pc.shape[0]
    tb = ATT_BLK
    nq = seq_len // tb
    nh = 1024 // HD
    fcol = 3 * 1024 // LANES
    cum, cumt = pl.pallas_call(
        _fox_cum_kernel,
        out_shape=[jax.ShapeDtypeStruct((m, LANES), F32), jax.ShapeDtypeStruct((batch * LANES, seq_len), F32)],
        grid=(batch,),
        in_specs=[pl.BlockSpec((seq_len, LANES), lambda b: (b, fcol)), pl.BlockSpec((1, LANES), lambda b: (0, 0))],
        out_specs=[pl.BlockSpec((seq_len, LANES), lambda b: (b, 0)), pl.BlockSpec((LANES, seq_len), lambda b: (b, 0))],
        compiler_params=_cparams(("parallel",)),
        name="fox_cum",
    )(pc, b_f_pad)
    return pl.pallas_call(
        _fox_kernel,
        out_shape=jax.ShapeDtypeStruct((m, 1024), BF16),
        grid=(batch, nh, nq),
        in_specs=[
            pl.BlockSpec((tb, HD), lambda b, h, i: (b * nq + i, h)),
            pl.BlockSpec((seq_len, HD), lambda b, h, i: (b, nh + h)),
            pl.BlockSpec((seq_len, HD), lambda b, h, i: (b, 2 * nh + h)),
            pl.BlockSpec((2, HD), lambda b, h, i: (0, 0)),
            pl.BlockSpec((tb, LANES), lambda b, h, i: (b * nq + i, 0)),
            pl.BlockSpec((SUBLANES, seq_len), lambda b, h, i: (b * (LANES // SUBLANES), 0)),
        ],
        out_specs=pl.BlockSpec((tb, HD), lambda b, h, i: (b * nq + i, h)),
        scratch_shapes=[pltpu.VMEM((seq_len, HD), BF16)],
        compiler_params=_cparams(("parallel", "parallel", "arbitrary")),
        name="forgetting_attention",
    )(pc, pc, pc, qk_g, cum, cumt)


DN_PD = 4224


def _dn_pre_kernel(seq_len, p_ref, halo_ref, cw_ref, gp_ref, q_ref, k_ref, v_ref, gb_ref):
    tr = p_ref.shape[0]
    w = 1024
    first = (pl.program_id(0) * tr) % seq_len == 0
    x = p_ref[:, 0:3 * w]
    halo = jnp.where(first, 0.0, halo_ref[:, 0:3 * w])
    acc = x * cw_ref[DN_CONV - 1:DN_CONV, :]
    for s in range(1, DN_CONV):
        acc = acc + _shift_rows(x, halo, s) * cw_ref[DN_CONV - 1 - s:DN_CONV - s, :]
    y = acc * _sigmoid(acc)
    for hh in range(w // HD):
        qs = y[:, hh * HD:(hh + 1) * HD]
        ks = y[:, w + hh * HD:w + (hh + 1) * HD]
        q_ref[:, hh * HD:(hh + 1) * HD] = qs * lax.rsqrt(jnp.sum(qs * qs, -1, keepdims=True) + L2_EPS) * (HD ** -0.5)
        k_ref[:, hh * HD:(hh + 1) * HD] = ks * lax.rsqrt(jnp.sum(ks * ks, -1, keepdims=True) + L2_EPS)
    v_ref[...] = y[:, 2 * w:3 * w]
    s = p_ref[:, 4 * w:4 * w + LANES]
    nh = w // HD
    beta = _sigmoid(s)
    g = -jnp.exp(gp_ref[0:1, :]) * _softplus(s + gp_ref[1:2, :])
    gb_ref[...] = jnp.where(_iota2(s.shape, 1) < nh, beta, g)


def _dn_scan_kernel(q_ref, k_ref, v_ref, z_ref, gb_ref, ng_ref, y_ref, s_ref):
    @pl.when(pl.program_id(1) == 0)
    def _():
        s_ref[...] = jnp.zeros_like(s_ref)

    c = DN_CHUNK
    nh = 1024 // HD
    gb = gb_ref[...]
    ltri = jnp.where(_tri(c, strict=False), 1.0, 0.0).astype(BF16)
    hi = gb.astype(BF16)
    mid = (gb - hi.astype(F32)).astype(BF16)
    lo = (gb - hi.astype(F32) - mid.astype(F32)).astype(BF16)
    gc = _dot(ltri, hi) + _dot(ltri, mid) + _dot(ltri, lo)
    gct = gc.T
    strict, incl = _tri(c, strict=True), _tri(c, strict=False)
    for h in range(nh):
        sl = slice(h * HD, (h + 1) * HD)
        q, k, v = q_ref[:, sl], k_ref[:, sl], v_ref[:, sl]
        beta = gb[:, h:h + 1]
        gcol = gc[:, nh + h:nh + h + 1]
        grow = gct[nh + h:nh + h + 1, :]
        glast = gc[c - 1:c, nh + h:nh + h + 1]
        decay = jnp.exp(jnp.minimum(gcol - grow, 0.0))
        kb = k * beta
        kbf = k.astype(BF16)
        gram = _dot_nt(jnp.concatenate([kb.astype(BF16), q.astype(BF16)], axis=0), kbf)
        lmat = jnp.where(strict, gram[:c] * decay, 0.0)
        attn = jnp.where(incl, gram[c:] * decay, 0.0)
        t_inv = _inv_unit_lower(-lmat, 6).astype(BF16)
        eg = jnp.exp(gcol)
        rhs = jnp.concatenate([(v * beta).astype(BF16), (kb * eg).astype(BF16)], axis=1)
        uw = _dot(t_inv, rhs)
        s0 = s_ref[h]
        s0b = s0.astype(BF16)
        ws = _dot(jnp.concatenate([uw[:, HD:].astype(BF16), (q * eg).astype(BF16)], axis=0), s0b)
        v_new = uw[:, :HD] - ws[:c]
        vnb = v_new.astype(BF16)
        o = ws[c:] + _dot(attn.astype(BF16), vnb)
        s_ref[h] = s0 * jnp.exp(glast) + _dot_tn((k * jnp.exp(glast - gcol)).astype(BF16), vnb)
        o = o * lax.rsqrt(jnp.mean(o * o, axis=-1, keepdims=True) + NORM_EPS) * ng_ref[...]
        z = z_ref[:, sl]
        y_ref[:, sl] = (o * (z * _sigmoid(z))).astype(BF16)


def _dn_mixer(pd, batch, seq_len, conv_w, gparams, norm_g, tr=256):
    m = pd.shape[0]
    w = 1024
    row = lambda i: (i, 0)
    fix = lambda i: (0, 0)
    wide = jax.ShapeDtypeStruct((m, w), F32)
    q, k, v, gb = pl.pallas_call(
        functools.partial(_dn_pre_kernel, seq_len),
        out_shape=[wide, wide, wide, jax.ShapeDtypeStruct((m, LANES), F32)],
        grid=(m // tr,),
        in_specs=[
            pl.BlockSpec((tr, DN_PD), row),
            pl.BlockSpec((SUBLANES, DN_PD), lambda i: (jnp.maximum(i * (tr // SUBLANES) - 1, 0), 0)),
            pl.BlockSpec((DN_CONV, 3 * w), fix),
            pl.BlockSpec((2, LANES), fix),
        ],
        out_specs=[pl.BlockSpec((tr, w), row)] * 3 + [pl.BlockSpec((tr, LANES), row)],
        compiler_params=_cparams(("parallel",)),
        name="deltanet_pre",
    )(pd, pd, conv_w, gparams)

    c = DN_CHUNK
    nchunk = seq_len // c
    blk = pl.BlockSpec((c, w), lambda bi, ci: (bi * nchunk + ci, 0))
    return pl.pallas_call(
        _dn_scan_kernel,
        out_shape=jax.ShapeDtypeStruct((m, w), BF16),
        grid=(batch, nchunk),
        in_specs=[blk, blk, blk,
                  pl.BlockSpec((c, w), lambda bi, ci: (bi * nchunk + ci, 3)),
                  pl.BlockSpec((c, LANES), lambda bi, ci: (bi * nchunk + ci, 0)),
                  pl.BlockSpec((1, HD), lambda bi, ci: (0, 0))],
        out_specs=blk,
        scratch_shapes=[pltpu.VMEM((w // HD, HD, HD), F32)],
        compiler_params=_cparams(("parallel", "arbitrary")),
        name="deltanet_scan",
    )(q, k, v, pd, gb, norm_g)


def _merge_kernel(hn_ref, ya_ref, yb_ref, yc_ref, yd_ref, wg_ref, bg_ref, wb_ref, o_ref, acc_ref):
    i = pl.program_id(1)

    @pl.when(i == 0)
    def _():
        acc_ref[...] = jnp.zeros_like(acc_ref)

    gate = _sigmoid(_dot(hn_ref[...], wg_ref[0]) + bg_ref[0])
    ys = (ya_ref, yb_ref, yc_ref, yd_ref)
    for n in range(4):
        @pl.when(i == n)
        def _(n=n):
            acc_ref[...] += gate * _dot(ys[n][...], wb_ref[0])

    @pl.when(i == 3)
    def _():
        o_ref[...] = acc_ref[...].astype(BF16)


def _outproj_kernel(x_ref, m_ref, w_ref, g_ref, o_ref):
    o_ref[...] = x_ref[...] + _rms(_dot(m_ref[...], w_ref[...]), g_ref[...])


def _merge(x, hn, g3, ys, w_gate, b_gate, w_branch, w_out, tm=512):
    m, d = x.shape
    bw = ys[0].shape[1]
    row = lambda r, i: (r, 0)
    merged = pl.pallas_call(
        _merge_kernel,
        out_shape=jax.ShapeDtypeStruct((m, d), BF16),
        grid=(m // tm, 4),
        in_specs=[
            pl.BlockSpec((tm, d), row),
            pl.BlockSpec((tm, bw), row),
            pl.BlockSpec((tm, bw), row),
            pl.BlockSpec((tm, bw), row),
            pl.BlockSpec((tm, bw), row),
            pl.BlockSpec((1, d, d), lambda r, i: (i, 0, 0)),
            pl.BlockSpec((1, 1, d), lambda r, i: (i, 0, 0)),
            pl.BlockSpec((1, bw, d), lambda r, i: (i, 0, 0)),
        ],
        out_specs=pl.BlockSpec((tm, d), row),
        scratch_shapes=[pltpu.VMEM((tm, d), F32)],
        compiler_params=_cparams(("parallel", "arbitrary")),
        name="merge",
    )(hn, ys[0], ys[1], ys[2], ys[3], w_gate, b_gate, w_branch)
    return pl.pallas_call(
        _outproj_kernel,
        out_shape=jax.ShapeDtypeStruct((m, d), F32),
        grid=(m // tm,),
        in_specs=[
            pl.BlockSpec((tm, d), lambda r: (r, 0)),
            pl.BlockSpec((tm, d), lambda r: (r, 0)),
            pl.BlockSpec((d, d), lambda r: (0, 0)),
            pl.BlockSpec((1, d), lambda r: (0, 0)),
        ],
        out_specs=pl.BlockSpec((tm, d), lambda r: (r, 0)),
        compiler_params=_cparams(("parallel",)),
        name="outproj",
    )(x, merged, w_out, g3)


def _pad_cols(w, n):
    return jnp.pad(w, ((0, 0), (0, n - w.shape[1])))


def _pad_rows(w, n):
    return jnp.pad(w, ((0, n - w.shape[0]), (0, 0)))


def kernel(x, norm_g, ffn_w_in, ffn_w_out, w_in, w_gate, b_gate, w_branch, w_out, rwkv_mu, rwkv_vec, rwkv_r_k,
           rwkv_w2, rwkv_a2, rwkv_g2, fox_qk_g, fox_b_f, dn_conv, dn_A_log, dn_dt_bias, dn_norm_g):
    batch, seq_len, d = x.shape
    depth = norm_g.shape[0]
    bw = w_branch.shape[2]
    rw_lo = rwkv_w2.shape[1]
    ra_lo = rwkv_a2.shape[1]
    rg_lo = rwkv_g2.shape[1]
    nh = bw // HD
    assert bw == 1024 and rw_lo <= LANES and ra_lo <= LANES and rg_lo == 2 * LANES
    assert seq_len % DN_CHUNK == 0 and (batch * seq_len) % 512 == 0

    rw_cols = 3 * bw + rw_lo + ra_lo + rg_lo
    sb_cols = 3 * bw
    fx_cols = 3 * bw + nh
    o_a, o_b, o_c, o_d = 0, rw_cols, rw_cols + sb_cols, rw_cols + sb_cols + fx_cols

    head_of = jnp.arange(bw) // RW_HD
    e_blk = (head_of[:, None] == head_of[None, :]).astype(BF16)

    xf = x.reshape(batch * seq_len, d)
    for l in range(depth):
        g = norm_g[l]
        wl = w_in[l]
        wa = jnp.concatenate([
            wl[:, o_a:o_a + 3 * bw],
            _pad_cols(wl[:, o_a + 3 * bw:o_a + 3 * bw + rw_lo], LANES),
            _pad_cols(wl[:, o_a + 3 * bw + rw_lo:o_a + 3 * bw + rw_lo + ra_lo], LANES),
            wl[:, o_a + 3 * bw + rw_lo + ra_lo:o_b]], axis=1).astype(BF16)
        wb = wl[:, o_b:o_c].astype(BF16)
        wc = _pad_cols(wl[:, o_c:o_d], 3 * bw + LANES).astype(BF16)
        wd = _pad_cols(wl[:, o_d:], DN_PD).astype(BF16)
        mu = rwkv_mu[l]
        mu_p = jnp.concatenate([
            mu[0:3 * bw], jnp.pad(mu[3 * bw:3 * bw + rw_lo], (0, LANES - rw_lo)),
            jnp.pad(mu[3 * bw + rw_lo:3 * bw + rw_lo + ra_lo], (0, LANES - ra_lo)),
            mu[3 * bw + rw_lo + ra_lo:]])[None, :]
        w2p = _pad_rows(rwkv_w2[l], LANES)
        a2p = _pad_rows(rwkv_a2[l], LANES)
        b_f_pad = jnp.pad(fox_b_f[l], (0, LANES - nh))[None, :]
        gparams = jnp.stack([jnp.pad(dn_A_log[l], (nh, LANES - 2 * nh)), jnp.pad(dn_dt_bias[l], (nh, LANES - 2 * nh))])

        xf = _ffn(xf, g[0:1], g[1:2], ffn_w_in[l, 0].astype(BF16), ffn_w_out[l, 0].astype(BF16))

        pa, hn = _normproj(xf, g[2:3], wa, tn=512, name="proj_rwkv")
        pb, _ = _normproj(xf, g[2:3], wb, tn=1024, name="proj_sb")
        pc, _ = _normproj(xf, g[2:3], wc, tn=640, name="proj_fox")
        pd, _ = _normproj(xf, g[2:3], wd, tn=1408, name="proj_dn")

        ya = _rwkv_mixer(pa, batch, seq_len, mu_p, rwkv_vec[l], rwkv_r_k[l].reshape(1, bw), w2p, a2p, rwkv_g2[l], e_blk)
        yb = _sb_mixer(pb, batch, seq_len)
        yc = _fox_mixer(pc, batch, seq_len, fox_qk_g[l], b_f_pad)
        yd = _dn_mixer(pd, batch, seq_len, dn_conv[l], gparams, dn_norm_g[l][None, :])

        xf = _merge(xf, hn, g[3:4], (ya, yb, yc, yd), w_gate[l].astype(BF16), b_gate[l][:, None, :],
                    w_branch[l].astype(BF16), w_out[l].astype(BF16))

        xf = _ffn(xf, g[4:5], g[5:6], ffn_w_in[l, 1].astype(BF16), ffn_w_out[l, 1].astype(BF16))
    return xf.reshape(batch, seq_len, d)
```

```python
import functools

import jax
import jax.numpy as jnp
from jax import lax
from jax.experimental import pallas as pl
from jax.experimental.pallas import tpu as pltpu

F32 = jnp.float32
BF16 = jnp.bfloat16

NORM_EPS = 1e-6
RW_GN_EPS = 64e-5
L2_EPS = 1e-12

LANES = 128
SUBLANES = 8
VMEM_LIMIT = 56 * 1024 * 1024

RW_HD = 64
RW_CHUNK = 64
HD = 128
DN_CHUNK = 128
DN_CONV = 4
ATT_BLK = 256
ATT_HG = 4


def _cparams(sem):
    return pltpu.CompilerParams(dimension_semantics=sem, vmem_limit_bytes=VMEM_LIMIT)


def _dot(a, b):
    return jnp.dot(a, b, preferred_element_type=F32)


def _dot_nt(a, b):
    return lax.dot_general(a, b, (((1,), (1,)), ((), ())), preferred_element_type=F32)


def _dot_tn(a, b):
    return lax.dot_general(a, b, (((0,), (0,)), ((), ())), preferred_element_type=F32)


def _split(x):
    hi = x.astype(BF16)
    lo = (x - hi.astype(F32)).astype(BF16)
    return hi, lo


def _dot_x2(a_exact, x):
    hi, lo = _split(x)
    return _dot(a_exact, hi) + _dot(a_exact, lo)


def _dot_2x(x, b_exact):
    hi, lo = _split(x)
    return _dot(hi, b_exact) + _dot(lo, b_exact)


def _dot_3(a, b):
    ah, al = _split(a)
    bh, bl = _split(b)
    return _dot(ah, bh) + _dot(ah, bl) + _dot(al, bh)


def _sigmoid(x):
    return 1.0 / (1.0 + jnp.exp(-x))


def _softplus(x):
    return jnp.maximum(x, 0.0) + jnp.log(1.0 + jnp.exp(-jnp.abs(x)))


def _rms(x, g):
    return x * lax.rsqrt(jnp.mean(x * x, axis=-1, keepdims=True) + NORM_EPS) * g


def _iota2(shape, dim):
    return lax.broadcasted_iota(jnp.int32, shape, dim)


def _tri(n, strict):
    r, c = _iota2((n, n), 0), _iota2((n, n), 1)
    return (c < r) if strict else (c <= r)


def _shift_rows(x, halo, k):
    rolled = pltpu.roll(x, k, axis=0)
    hx = pltpu.roll(halo, k, axis=0)
    top = jnp.where(_iota2(hx.shape, 0) < k, hx, rolled[0:SUBLANES])
    return jnp.concatenate([top, rolled[SUBLANES:]], axis=0)


def _inv_unit_lower_multi(n_mats, steps):
    n = n_mats[0].shape[0]
    eye = jnp.where(_iota2((n, n), 0) == _iota2((n, n), 1), 1.0, 0.0)
    ps = [eye + nm for nm in n_mats]
    ms = list(n_mats)
    for _ in range(steps):
        mbs = [m.astype(BF16) for m in ms]
        ms = [_dot(mb, mb) for mb in mbs]
        ps = [p + _dot(p.astype(BF16), m.astype(BF16)) for p, m in zip(ps, ms)]
    res = [(eye - p) + _dot_3(nm, p) for nm, p in zip(n_mats, ps)]
    return [p + _dot(p.astype(BF16), r.astype(BF16)) for p, r in zip(ps, res)]


def _inv_unit_lower(n_mat, steps):
    return _inv_unit_lower_multi([n_mat], steps)[0]


def _ffn_kernel(x_ref, g0_ref, g1_ref, wg_ref, wu_ref, wo_ref, o_ref, xn_ref, acc_ref):
    j = pl.program_id(1)

    @pl.when(j == 0)
    def _():
        xn_ref[...] = _rms(x_ref[...], g0_ref[...]).astype(BF16)
        acc_ref[...] = jnp.zeros_like(acc_ref)

    xn = xn_ref[...]
    gate = _dot(xn, wg_ref[...])
    up = _dot(xn, wu_ref[...])
    act = (gate * _sigmoid(gate) * up).astype(BF16)
    acc_ref[...] += _dot(act, wo_ref[...])

    @pl.when(j == pl.num_programs(1) - 1)
    def _():
        o_ref[...] = x_ref[...] + 0.5 * _rms(acc_ref[...], g1_ref[...])


def _ffn(x, g0, g1, w_in, w_out, tm=512, tf=512):
    m, d = x.shape
    dff = w_out.shape[0]
    nf = dff // tf
    return pl.pallas_call(
        _ffn_kernel,
        out_shape=jax.ShapeDtypeStruct((m, d), F32),
        grid=(m // tm, nf),
        in_specs=[
            pl.BlockSpec((tm, d), lambda i, j: (i, 0)),
            pl.BlockSpec((1, d), lambda i, j: (0, 0)),
            pl.BlockSpec((1, d), lambda i, j: (0, 0)),
            pl.BlockSpec((d, tf), lambda i, j: (0, j)),
            pl.BlockSpec((d, tf), lambda i, j: (0, j + nf)),
            pl.BlockSpec((tf, d), lambda i, j: (j, 0)),
        ],
        out_specs=pl.BlockSpec((tm, d), lambda i, j: (i, 0)),
        scratch_shapes=[pltpu.VMEM((tm, d), BF16), pltpu.VMEM((tm, d), F32)],
        compiler_params=_cparams(("parallel", "arbitrary")),
        name="ffn",
    )(x, g0, g1, w_in, w_in, w_out)


def _normproj_kernel(x_ref, g_ref, w_ref, o_ref, xn_ref):
    @pl.when(pl.program_id(1) == 0)
    def _():
        xn_ref[...] = _rms(x_ref[...], g_ref[...]).astype(BF16)

    o_ref[...] = _dot(xn_ref[...], w_ref[...])


def _normproj(x, g, w, tn, tm=512, name="normproj"):
    m, d = x.shape
    n = w.shape[1]
    return pl.pallas_call(
        _normproj_kernel,
        out_shape=[jax.ShapeDtypeStruct((m, n), F32), jax.ShapeDtypeStruct((m, d), BF16)],
        grid=(m // tm, n // tn),
        in_specs=[
            pl.BlockSpec((tm, d), lambda i, j: (i, 0)),
            pl.BlockSpec((1, d), lambda i, j: (0, 0)),
            pl.BlockSpec((d, tn), lambda i, j: (0, j)),
        ],
        out_specs=[pl.BlockSpec((tm, tn), lambda i, j: (i, j)), pl.BlockSpec((tm, d), lambda i, j: (i, 0))],
        compiler_params=_cparams(("parallel", "arbitrary")),
        name=name,
    )(x, g, w)


RW_W = 1024
RW_PA = 3584


def _rwkv_pre_kernel(seq_len, p_ref, halo_ref, mu_ref, vec_ref, rk_ref, w2_ref, a2_ref, g2_ref, e_ref,
                     r_ref, lw_ref, k_ref, v_ref, nkk_ref, b_ref, g_ref, bonus_ref):
    tr = p_ref.shape[0]
    x = p_ref[...]
    first = (pl.program_id(0) * tr) % seq_len == 0
    halo = jnp.where(first, 0.0, halo_ref[...])
    prev = _shift_rows(x, halo, 1)
    x = x + (prev - x) * mu_ref[...]
    w = RW_W
    r, k, v = x[:, 0:w], x[:, w:2 * w], x[:, 2 * w:3 * w]
    w_lo, a_lo, g_lo = x[:, 3 * w:3 * w + 128], x[:, 3 * w + 128:3 * w + 256], x[:, 3 * w + 256:3 * w + 512]
    w0, a0, k_k, k_a = vec_ref[0:1, :], vec_ref[1:2, :], vec_ref[2:3, :], vec_ref[3:4, :]
    lw = -jnp.exp(-0.5) * _sigmoid(w0 + _dot_3(jnp.tanh(w_lo), w2_ref[...]))
    a = _sigmoid(a0 + _dot_3(a_lo, a2_ref[...]))
    g = _dot_3(_sigmoid(g_lo), g2_ref[...])
    e = e_ref[...]
    kk = k * k_k
    kk = kk * lax.rsqrt(_dot_2x(kk * kk, e) + L2_EPS)
    k = k * (1.0 + (a - 1.0) * k_a)
    bonus = _dot_2x(r * k * rk_ref[...], e) * v
    r_ref[...] = r
    lw_ref[...] = lw
    k_ref[...] = k
    v_ref[...] = v
    nkk_ref[...] = -kk
    b_ref[...] = kk * a
    g_ref[...] = g
    bonus_ref[...] = bonus


def _rwkv_scan_kernel(r_ref, lw_ref, k_ref, v_ref, nkk_ref, b_ref, o_ref, s_ref):
    @pl.when(pl.program_id(1) == 0)
    def _():
        s_ref[...] = jnp.zeros_like(s_ref)

    c = RW_CHUNK
    lw = lw_ref[...]
    ltri = jnp.where(_tri(c, strict=False), 1.0, 0.0).astype(BF16)
    cum = _dot_x2(ltri, lw)
    tot = cum[c - 1:c, :]
    e_in, e_neg = jnp.exp(cum), jnp.exp(-cum)
    e_ex, e_end = jnp.exp(cum - lw), jnp.exp(tot - cum)
    r, k, v, nkk, b = r_ref[...], k_ref[...], v_ref[...], nkk_ref[...], b_ref[...]
    rt, kt, bt, at = r * e_in, k * e_neg, b * e_neg, nkk * e_ex
    kh, bh = k * e_end, b * e_end
    e_tot = jnp.exp(tot)

    n2 = 2 * c
    lane_lo = _iota2((c, LANES), 1) < RW_HD
    ri, ci = _iota2((n2, n2), 0) & (c - 1), _iota2((n2, n2), 1) & (c - 1)
    strict, incl = ci < ri, ci <= ri

    def stack(xp):
        return jnp.concatenate([jnp.where(lane_lo, xp, 0.0), jnp.where(lane_lo, 0.0, xp)], axis=0).astype(BF16)

    pairs = range(RW_W // LANES)
    sls = [slice(pr * LANES, (pr + 1) * LANES) for pr in pairs]
    at_s = [stack(at[:, sl]) for sl in sls]
    rt_s = [stack(rt[:, sl]) for sl in sls]
    bt_s = [stack(bt[:, sl]) for sl in sls]
    kt_s = [stack(kt[:, sl]) for sl in sls]
    v_s = [stack(v[:, sl]) for sl in sls]
    gram = [_dot_nt(jnp.concatenate([at_s[p], rt_s[p]], axis=0), jnp.concatenate([bt_s[p], kt_s[p]], axis=0))
            for p in pairs]
    t_inv = _inv_unit_lower_multi([jnp.where(strict, gm[:n2, :n2], 0.0) for gm in gram], 5)
    akv = [_dot(jnp.where(strict, gram[p][:n2, n2:], 0.0).astype(BF16), v_s[p]) for p in pairs]
    wu = [_dot(t_inv[p].astype(BF16), jnp.concatenate([at_s[p], akv[p].astype(BF16)], axis=1))
          for p in pairs]
    s0 = [s_ref[p] for p in pairs]
    ws = [_dot_nt(jnp.concatenate([wu[p][:, :LANES].astype(BF16), rt_s[p]], axis=0), s0[p].astype(BF16))
          for p in pairs]
    uv = [jnp.concatenate([(ws[p][:n2] + wu[p][:, LANES:]).astype(BF16), v_s[p]], axis=0) for p in pairs]
    for p in pairs:
        a_r = jnp.concatenate([jnp.where(incl, gram[p][n2:, :n2], 0.0), jnp.where(incl, gram[p][n2:, n2:], 0.0)], axis=1)
        o_s = ws[p][n2:] + _dot(a_r.astype(BF16), uv[p])
        o_ref[:, sls[p]] = o_s[:c] + o_s[c:]
    for p in pairs:
        bk = jnp.concatenate([stack(bh[:, sls[p]]), stack(kh[:, sls[p]])], axis=0)
        s_ref[p] = s0[p] * e_tot[:, sls[p]] + _dot_tn(uv[p], bk)


def _rwkv_post_kernel(o_ref, bonus_ref, g_ref, ln_ref, e_ref, y_ref):
    o = o_ref[...]
    e = e_ref[...]
    mean = _dot_2x(o, e) * (1.0 / RW_HD)
    d = o - mean
    var = _dot_2x(d * d, e) * (1.0 / RW_HD)
    o = d * lax.rsqrt(var + RW_GN_EPS) * ln_ref[0:1, :] + ln_ref[1:2, :]
    y_ref[...] = ((o + bonus_ref[...]) * g_ref[...]).astype(BF16)


def _rwkv_mixer(pa, batch, seq_len, mu, vec, r_k, w2, a2, g2, e_blk, tr=256):
    m = pa.shape[0]
    w = RW_W
    row = lambda i: (i, 0)
    fix = lambda i: (0, 0)
    wide = jax.ShapeDtypeStruct((m, w), F32)
    r, lw, k, v, nkk, b, g, bonus = pl.pallas_call(
        functools.partial(_rwkv_pre_kernel, seq_len),
        out_shape=[wide] * 8,
        grid=(m // tr,),
        in_specs=[
            pl.BlockSpec((tr, RW_PA), row),
            pl.BlockSpec((SUBLANES, RW_PA), lambda i: (jnp.maximum(i * (tr // SUBLANES) - 1, 0), 0)),
            pl.BlockSpec((1, RW_PA), fix),
            pl.BlockSpec((4, w), fix),
            pl.BlockSpec((1, w), fix),
            pl.BlockSpec((LANES, w), fix),
            pl.BlockSpec((LANES, w), fix),
            pl.BlockSpec((2 * LANES, w), fix),
            pl.BlockSpec((w, w), fix),
        ],
        out_specs=[pl.BlockSpec((tr, w), row)] * 8,
        compiler_params=_cparams(("parallel",)),
        name="rwkv_pre",
    )(pa, pa, mu, vec[0:4], r_k, w2, a2, g2, e_blk)

    c = RW_CHUNK
    nchunk = seq_len // c
    blk = pl.BlockSpec((c, w), lambda bi, ci: (bi * nchunk + ci, 0))
    o = pl.pallas_call(
        _rwkv_scan_kernel,
        out_shape=wide,
        grid=(batch, nchunk),
        in_specs=[blk] * 6,
        out_specs=blk,
        scratch_shapes=[pltpu.VMEM((w // LANES, LANES, LANES), F32)],
        compiler_params=_cparams(("parallel", "arbitrary")),
        name="rwkv_scan",
    )(r, lw, k, v, nkk, b)

    return pl.pallas_call(
        _rwkv_post_kernel,
        out_shape=jax.ShapeDtypeStruct((m, w), BF16),
        grid=(m // tr,),
        in_specs=[pl.BlockSpec((tr, w), row)] * 3 + [pl.BlockSpec((2, w), fix), pl.BlockSpec((w, w), fix)],
        out_specs=pl.BlockSpec((tr, w), row),
        compiler_params=_cparams(("parallel",)),
        name="rwkv_post",
    )(o, bonus, g, vec[4:6], e_blk)


def _sb_kernel(q_ref, k_ref, v_ref, o_ref, kb_ref, vb_ref):
    i = pl.program_id(2)
    tb = ATT_BLK
    heads = range(ATT_HG)
    hsl = [slice(hh * HD, (hh + 1) * HD) for hh in heads]

    @pl.when(i == 0)
    def _():
        kb_ref[...] = k_ref[...].astype(BF16)
        vb_ref[...] = v_ref[...].astype(BF16)

    q = [(q_ref[:, sl] * (HD ** -0.5)).astype(BF16) for sl in hsl]
    r, c = _iota2((tb, tb), 0), _iota2((tb, tb), 1)
    upper = jnp.where(r > c, 1.0, 0.0).astype(BF16)
    diag_ok = c < r

    def block(j, carry, acc, masked):
        rows = pl.ds(pl.multiple_of(j * tb, tb), tb)
        z = [_dot_nt(q[hh], kb_ref[rows, hsl[hh]]) for hh in heads]
        sp = [_softplus(zz) for zz in z]
        lk = [jnp.where(diag_ok, -s, 0.0) if masked else -s for s in sp]
        later = [_dot_2x(lk[hh], upper) + carry[hh] for hh in heads]
        a = [jnp.exp(z[hh] - sp[hh] + later[hh]) for hh in heads]
        if masked:
            a = [jnp.where(diag_ok, aa, 0.0) for aa in a]
        acc = [acc[hh] + _dot(a[hh].astype(BF16), vb_ref[rows, hsl[hh]]) for hh in heads]
        carry = [later[hh][:, 0:1] + lk[hh][:, 0:1] for hh in heads]
        return carry, acc

    st = block(i, [jnp.zeros((tb, 1), F32)] * ATT_HG, [jnp.zeros((tb, HD), F32)] * ATT_HG, True)

    def body(n, st):
        return block(i - 1 - n, st[0], st[1], False)

    carry, acc = lax.fori_loop(0, i, body, st)
    for hh in heads:
        o_ref[:, hsl[hh]] = acc[hh].astype(BF16)


def _sb_mixer(pb, batch, seq_len):
    m = pb.shape[0]
    tb = ATT_BLK
    nq = seq_len // tb
    ng = 1024 // (HD * ATT_HG)
    gw = HD * ATT_HG
    return pl.pallas_call(
        _sb_kernel,
        out_shape=jax.ShapeDtypeStruct((m, 1024), BF16),
        grid=(batch, ng, nq),
        in_specs=[
            pl.BlockSpec((tb, gw), lambda b, h, i: (b * nq + i, h)),
            pl.BlockSpec((seq_len, gw), lambda b, h, i: (b, ng + h)),
            pl.BlockSpec((seq_len, gw), lambda b, h, i: (b, 2 * ng + h)),
        ],
        out_specs=pl.BlockSpec((tb, gw), lambda b, h, i: (b * nq + i, h)),
        scratch_shapes=[pltpu.VMEM((seq_len, gw), BF16), pltpu.VMEM((seq_len, gw), BF16)],
        compiler_params=_cparams(("parallel", "parallel", "arbitrary")),
        name="stick_breaking",
    )(pb, pb, pb)


def _fox_cum_kernel(f_ref, bf_ref, cum_ref, cumt_ref):
    tb = LANES
    x = f_ref[...] + bf_ref[...]
    lf = jnp.minimum(x, 0.0) - jnp.log(1.0 + jnp.exp(-jnp.abs(x)))
    ltri = jnp.where(_tri(tb, strict=False), 1.0, 0.0).astype(BF16)
    carry = jnp.zeros((1, LANES), F32)
    for blk in range(f_ref.shape[0] // tb):
        xb = lf[blk * tb:(blk + 1) * tb, :]
        hi = xb.astype(BF16)
        mid = (xb - hi.astype(F32)).astype(BF16)
        lo = (xb - hi.astype(F32) - mid.astype(F32)).astype(BF16)
        cb = _dot(ltri, hi) + _dot(ltri, mid) + _dot(ltri, lo) + carry
        cum_ref[blk * tb:(blk + 1) * tb, :] = cb
        cumt_ref[:, blk * tb:(blk + 1) * tb] = cb.T
        carry = cb[tb - 1:tb, :]


def _fox_kernel(q_ref, k_ref, v_ref, g_ref, cq_ref, ck_ref, o_ref, kn_ref, vb_ref):
    hg = pl.program_id(1)
    i = pl.program_id(2)
    tb = ATT_BLK
    heads = range(ATT_HG)
    hsl = [slice(hh * HD, (hh + 1) * HD) for hh in heads]

    def hnorm(x, g):
        return x * lax.rsqrt(jnp.mean(x * x, axis=-1, keepdims=True) + NORM_EPS) * g

    @pl.when(i == 0)
    def _():
        for sl in hsl:
            kn_ref[:, sl] = hnorm(k_ref[:, sl], g_ref[1:2, :]).astype(BF16)
        vb_ref[...] = v_ref[...].astype(BF16)

    q = [(hnorm(q_ref[:, sl], g_ref[0:1, :]) * (HD ** -0.5)).astype(BF16) for sl in hsl]
    cq_all = cq_ref[...]
    lane = _iota2((tb, LANES), 1)
    cq = [jnp.sum(jnp.where(lane == hg * ATT_HG + hh, cq_all, 0.0), axis=-1, keepdims=True) for hh in heads]
    r, c = _iota2((tb, tb), 0), _iota2((tb, tb), 1)
    causal = c <= r

    def block(j, m_run, l_run, acc, masked):
        rows = pl.ds(pl.multiple_of(j * tb, tb), tb)
        ck8 = ck_ref[:, rows]
        sub = _iota2(ck8.shape, 0)
        ck = [jnp.sum(jnp.where(sub == hg * ATT_HG + hh, ck8, 0.0), axis=0, keepdims=True) for hh in heads]
        s = [_dot_nt(q[hh], kn_ref[rows, hsl[hh]]) + (cq[hh] - ck[hh]) for hh in heads]
        if masked:
            s = [jnp.where(causal, ss, -jnp.inf) for ss in s]
        m_new = [jnp.maximum(m_run[hh], jnp.max(s[hh], axis=-1, keepdims=True)) for hh in heads]
        alpha = [jnp.exp(m_run[hh] - m_new[hh]) for hh in heads]
        p = [jnp.exp(s[hh] - m_new[hh]) for hh in heads]
        l_new = [alpha[hh] * l_run[hh] + jnp.sum(p[hh], axis=-1, keepdims=True) for hh in heads]
        acc = [alpha[hh] * acc[hh] + _dot(p[hh].astype(BF16), vb_ref[rows, hsl[hh]]) for hh in heads]
        return m_new, l_new, acc

    st = block(i, [jnp.full((tb, 1), -jnp.inf, F32)] * ATT_HG, [jnp.zeros((tb, 1), F32)] * ATT_HG,
               [jnp.zeros((tb, HD), F32)] * ATT_HG, True)

    def body(n, st):
        return block(i - 1 - n, st[0], st[1], st[2], False)

    m_run, l_run, acc = lax.fori_loop(0, i, body, st)
    for hh in heads:
        o_ref[:, hsl[hh]] = (acc[hh] / l_run[hh]).astype(BF16)


def _fox_mixer(pc, batch, seq_len, qk_g, b_f_pad):
    m = pc.shape[0]
    tb = ATT_BLK
    nq = seq_len // tb
    nh = 1024 // HD
    fcol = 3 * 1024 // LANES
    cum, cumt = pl.pallas_call(
        _fox_cum_kernel,
        out_shape=[jax.ShapeDtypeStruct((m, LANES), F32), jax.ShapeDtypeStruct((batch * LANES, seq_len), F32)],
        grid=(batch,),
        in_specs=[pl.BlockSpec((seq_len, LANES), lambda b: (b, fcol)), pl.BlockSpec((1, LANES), lambda b: (0, 0))],
        out_specs=[pl.BlockSpec((seq_len, LANES), lambda b: (b, 0)), pl.BlockSpec((LANES, seq_len), lambda b: (b, 0))],
        compiler_params=_cparams(("parallel",)),
        name="fox_cum",
    )(pc, b_f_pad)
    ng = 1024 // (HD * ATT_HG)
    gw = HD * ATT_HG
    return pl.pallas_call(
        _fox_kernel,
        out_shape=jax.ShapeDtypeStruct((m, 1024), BF16),
        grid=(batch, ng, nq),
        in_specs=[
            pl.BlockSpec((tb, gw), lambda b, h, i: (b * nq + i, h)),
            pl.BlockSpec((seq_len, gw), lambda b, h, i: (b, ng + h)),
            pl.BlockSpec((seq_len, gw), lambda b, h, i: (b, 2 * ng + h)),
            pl.BlockSpec((2, HD), lambda b, h, i: (0, 0)),
            pl.BlockSpec((tb, LANES), lambda b, h, i: (b * nq + i, 0)),
            pl.BlockSpec((SUBLANES, seq_len), lambda b, h, i: (b * (LANES // SUBLANES), 0)),
        ],
        out_specs=pl.BlockSpec((tb, gw), lambda b, h, i: (b * nq + i, h)),
        scratch_shapes=[pltpu.VMEM((seq_len, gw), BF16), pltpu.VMEM((seq_len, gw), BF16)],
        compiler_params=_cparams(("parallel", "parallel", "arbitrary")),
        name="forgetting_attention",
    )(pc, pc, pc, qk_g, cum, cumt)


DN_PD = 4224


def _dn_pre_kernel(seq_len, p_ref, halo_ref, cw_ref, gp_ref, q_ref, k_ref, v_ref, gb_ref):
    tr = p_ref.shape[0]
    w = 1024
    first = (pl.program_id(0) * tr) % seq_len == 0
    x = p_ref[:, 0:3 * w]
    halo = jnp.where(first, 0.0, halo_ref[:, 0:3 * w])
    acc = x * cw_ref[DN_CONV - 1:DN_CONV, :]
    for s in range(1, DN_CONV):
        acc = acc + _shift_rows(x, halo, s) * cw_ref[DN_CONV - 1 - s:DN_CONV - s, :]
    y = acc * _sigmoid(acc)
    for hh in range(w // HD):
        qs = y[:, hh * HD:(hh + 1) * HD]
        ks = y[:, w + hh * HD:w + (hh + 1) * HD]
        q_ref[:, hh * HD:(hh + 1) * HD] = qs * lax.rsqrt(jnp.sum(qs * qs, -1, keepdims=True) + L2_EPS) * (HD ** -0.5)
        k_ref[:, hh * HD:(hh + 1) * HD] = ks * lax.rsqrt(jnp.sum(ks * ks, -1, keepdims=True) + L2_EPS)
    v_ref[...] = y[:, 2 * w:3 * w]
    s = p_ref[:, 4 * w:4 * w + LANES]
    nh = w // HD
    beta = _sigmoid(s)
    g = -jnp.exp(gp_ref[0:1, :]) * _softplus(s + gp_ref[1:2, :])
    gb_ref[...] = jnp.where(_iota2(s.shape, 1) < nh, beta, g)


def _dn_scan_kernel(q_ref, k_ref, v_ref, z_ref, gb_ref, ng_ref, y_ref, s_ref):
    @pl.when(pl.program_id(1) == 0)
    def _():
        s_ref[...] = jnp.zeros_like(s_ref)

    c = DN_CHUNK
    nh = 1024 // HD
    gb = gb_ref[...]
    ltri = jnp.where(_tri(c, strict=False), 1.0, 0.0).astype(BF16)
    hi = gb.astype(BF16)
    mid = (gb - hi.astype(F32)).astype(BF16)
    lo = (gb - hi.astype(F32) - mid.astype(F32)).astype(BF16)
    gc = _dot(ltri, hi) + _dot(ltri, mid) + _dot(ltri, lo)
    gct = gc.T
    strict, incl = _tri(c, strict=True), _tri(c, strict=False)
    heads = range(nh)
    sls = [slice(h * HD, (h + 1) * HD) for h in heads]
    q = [q_ref[:, sl] for sl in sls]
    k = [k_ref[:, sl] for sl in sls]
    beta = [gb[:, h:h + 1] for h in heads]
    gcol = [gc[:, nh + h:nh + h + 1] for h in heads]
    glast = [gc[c - 1:c, nh + h:nh + h + 1] for h in heads]
    eg = [jnp.exp(gcol[h]) for h in heads]
    decay = [jnp.exp(jnp.minimum(gcol[h] - gct[nh + h:nh + h + 1, :], 0.0)) for h in heads]
    kb = [k[h] * beta[h] for h in heads]
    gram = [_dot_nt(jnp.concatenate([kb[h].astype(BF16), q[h].astype(BF16)], axis=0), k[h].astype(BF16))
            for h in heads]
    t_inv = _inv_unit_lower_multi([jnp.where(strict, -gram[h][:c] * decay[h], 0.0) for h in heads], 6)
    uw = [_dot(t_inv[h].astype(BF16),
               jnp.concatenate([(v_ref[:, sls[h]] * beta[h]).astype(BF16), (kb[h] * eg[h]).astype(BF16)], axis=1))
          for h in heads]
    s0 = [s_ref[h] for h in heads]
    ws = [_dot(jnp.concatenate([uw[h][:, HD:].astype(BF16), (q[h] * eg[h]).astype(BF16)], axis=0), s0[h].astype(BF16))
          for h in heads]
    vnb = [(uw[h][:, :HD] - ws[h][:c]).astype(BF16) for h in heads]
    for h in heads:
        attn = jnp.where(incl, gram[h][c:] * decay[h], 0.0)
        o = ws[h][c:] + _dot(attn.astype(BF16), vnb[h])
        o = o * lax.rsqrt(jnp.mean(o * o, axis=-1, keepdims=True) + NORM_EPS) * ng_ref[...]
        z = z_ref[:, sls[h]]
        y_ref[:, sls[h]] = (o * (z * _sigmoid(z))).astype(BF16)
    for h in heads:
        s_ref[h] = s0[h] * jnp.exp(glast[h]) + _dot_tn((k[h] * jnp.exp(glast[h] - gcol[h])).astype(BF16), vnb[h])


def _dn_mixer(pd, batch, seq_len, conv_w, gparams, norm_g, tr=256):
    m = pd.shape[0]
    w = 1024
    row = lambda i: (i, 0)
    fix = lambda i: (0, 0)
    wide = jax.ShapeDtypeStruct((m, w), F32)
    q, k, v, gb = pl.pallas_call(
        functools.partial(_dn_pre_kernel, seq_len),
        out_shape=[wide, wide, wide, jax.ShapeDtypeStruct((m, LANES), F32)],
        grid=(m // tr,),
        in_specs=[
            pl.BlockSpec((tr, DN_PD), row),
            pl.BlockSpec((SUBLANES, DN_PD), lambda i: (jnp.maximum(i * (tr // SUBLANES) - 1, 0), 0)),
            pl.BlockSpec((DN_CONV, 3 * w), fix),
            pl.BlockSpec((2, LANES), fix),
        ],
        out_specs=[pl.BlockSpec((tr, w), row)] * 3 + [pl.BlockSpec((tr, LANES), row)],
        compiler_params=_cparams(("parallel",)),
        name="deltanet_pre",
    )(pd, pd, conv_w, gparams)

    c = DN_CHUNK
    nchunk = seq_len // c
    blk = pl.BlockSpec((c, w), lambda bi, ci: (bi * nchunk + ci, 0))
    return pl.pallas_call(
        _dn_scan_kernel,
        out_shape=jax.ShapeDtypeStruct((m, w), BF16),
        grid=(batch, nchunk),
        in_specs=[blk, blk, blk,
                  pl.BlockSpec((c, w), lambda bi, ci: (bi * nchunk + ci, 3)),
                  pl.BlockSpec((c, LANES), lambda bi, ci: (bi * nchunk + ci, 0)),
                  pl.BlockSpec((1, HD), lambda bi, ci: (0, 0))],
        out_specs=blk,
        scratch_shapes=[pltpu.VMEM((w // HD, HD, HD), F32)],
        compiler_params=_cparams(("parallel", "arbitrary")),
        name="deltanet_scan",
    )(q, k, v, pd, gb, norm_g)


def _merge_kernel(hn_ref, ya_ref, yb_ref, yc_ref, yd_ref, wg_ref, bg_ref, wb_ref, o_ref, acc_ref):
    i = pl.program_id(1)

    @pl.when(i == 0)
    def _():
        acc_ref[...] = jnp.zeros_like(acc_ref)

    gate = _sigmoid(_dot(hn_ref[...], wg_ref[0]) + bg_ref[0])
    ys = (ya_ref, yb_ref, yc_ref, yd_ref)
    for n in range(4):
        @pl.when(i == n)
        def _(n=n):
            acc_ref[...] += gate * _dot(ys[n][...], wb_ref[0])

    @pl.when(i == 3)
    def _():
        o_ref[...] = acc_ref[...].astype(BF16)


def _outproj_kernel(x_ref, m_ref, w_ref, g_ref, o_ref):
    o_ref[...] = x_ref[...] + _rms(_dot(m_ref[...], w_ref[...]), g_ref[...])


def _merge(x, hn, g3, ys, w_gate, b_gate, w_branch, w_out, tm=512):
    m, d = x.shape
    bw = ys[0].shape[1]
    row = lambda r, i: (r, 0)
    merged = pl.pallas_call(
        _merge_kernel,
        out_shape=jax.ShapeDtypeStruct((m, d), BF16),
        grid=(m // tm, 4),
        in_specs=[
            pl.BlockSpec((tm, d), row),
            pl.BlockSpec((tm, bw), row),
            pl.BlockSpec((tm, bw), row),
            pl.BlockSpec((tm, bw), row),
            pl.BlockSpec((tm, bw), row),
            pl.BlockSpec((1, d, d), lambda r, i: (i, 0, 0)),
            pl.BlockSpec((1, 1, d), lambda r, i: (i, 0, 0)),
            pl.BlockSpec((1, bw, d), lambda r, i: (i, 0, 0)),
        ],
        out_specs=pl.BlockSpec((tm, d), row),
        scratch_shapes=[pltpu.VMEM((tm, d), F32)],
        compiler_params=_cparams(("parallel", "arbitrary")),
        name="merge",
    )(hn, ys[0], ys[1], ys[2], ys[3], w_gate, b_gate, w_branch)
    return pl.pallas_call(
        _outproj_kernel,
        out_shape=jax.ShapeDtypeStruct((m, d), F32),
        grid=(m // tm,),
        in_specs=[
            pl.BlockSpec((tm, d), lambda r: (r, 0)),
            pl.BlockSpec((tm, d), lambda r: (r, 0)),
            pl.BlockSpec((d, d), lambda r: (0, 0)),
            pl.BlockSpec((1, d), lambda r: (0, 0)),
        ],
        out_specs=pl.BlockSpec((tm, d), lambda r: (r, 0)),
        compiler_params=_cparams(("parallel",)),
        name="outproj",
    )(x, merged, w_out, g3)


def _pad_cols(w, n):
    return jnp.pad(w, ((0, 0), (0, n - w.shape[1])))


def _pad_rows(w, n):
    return jnp.pad(w, ((0, n - w.shape[0]), (0, 0)))


def kernel(x, norm_g, ffn_w_in, ffn_w_out, w_in, w_gate, b_gate, w_branch, w_out, rwkv_mu, rwkv_vec, rwkv_r_k,
           rwkv_w2, rwkv_a2, rwkv_g2, fox_qk_g, fox_b_f, dn_conv, dn_A_log, dn_dt_bias, dn_norm_g):
    batch, seq_len, d = x.shape
    depth = norm_g.shape[0]
    bw = w_branch.shape[2]
    rw_lo = rwkv_w2.shape[1]
    ra_lo = rwkv_a2.shape[1]
    rg_lo = rwkv_g2.shape[1]
    nh = bw // HD
    assert bw == 1024 and rw_lo <= LANES and ra_lo <= LANES and rg_lo == 2 * LANES
    assert seq_len % DN_CHUNK == 0 and (batch * seq_len) % 512 == 0

    rw_cols = 3 * bw + rw_lo + ra_lo + rg_lo
    sb_cols = 3 * bw
    fx_cols = 3 * bw + nh
    o_a, o_b, o_c, o_d = 0, rw_cols, rw_cols + sb_cols, rw_cols + sb_cols + fx_cols

    head_of = jnp.arange(bw) // RW_HD
    e_blk = (head_of[:, None] == head_of[None, :]).astype(BF16)

    xf = x.reshape(batch * seq_len, d)
    for l in range(depth):
        g = norm_g[l]
        wl = w_in[l]
        wa = jnp.concatenate([
            wl[:, o_a:o_a + 3 * bw],
            _pad_cols(wl[:, o_a + 3 * bw:o_a + 3 * bw + rw_lo], LANES),
            _pad_cols(wl[:, o_a + 3 * bw + rw_lo:o_a + 3 * bw + rw_lo + ra_lo], LANES),
            wl[:, o_a + 3 * bw + rw_lo + ra_lo:o_b]], axis=1).astype(BF16)
        wb = wl[:, o_b:o_c].astype(BF16)
        wc = _pad_cols(wl[:, o_c:o_d], 3 * bw + LANES).astype(BF16)
        wd = _pad_cols(wl[:, o_d:], DN_PD).astype(BF16)
        mu = rwkv_mu[l]
        mu_p = jnp.concatenate([
            mu[0:3 * bw], jnp.pad(mu[3 * bw:3 * bw + rw_lo], (0, LANES - rw_lo)),
            jnp.pad(mu[3 * bw + rw_lo:3 * bw + rw_lo + ra_lo], (0, LANES - ra_lo)),
            mu[3 * bw + rw_lo + ra_lo:]])[None, :]
        w2p = _pad_rows(rwkv_w2[l], LANES)
        a2p = _pad_rows(rwkv_a2[l], LANES)
        b_f_pad = jnp.pad(fox_b_f[l], (0, LANES - nh))[None, :]
        gparams = jnp.stack([jnp.pad(dn_A_log[l], (nh, LANES - 2 * nh)), jnp.pad(dn_dt_bias[l], (nh, LANES - 2 * nh))])

        xf = _ffn(xf, g[0:1], g[1:2], ffn_w_in[l, 0].astype(BF16), ffn_w_out[l, 0].astype(BF16))

        pa, hn = _normproj(xf, g[2:3], wa, tn=512, name="proj_rwkv")
        pb, _ = _normproj(xf, g[2:3], wb, tn=1024, name="proj_sb")
        pc, _ = _normproj(xf, g[2:3], wc, tn=640, name="proj_fox")
        pd, _ = _normproj(xf, g[2:3], wd, tn=1408, name="proj_dn")

        ya = _rwkv_mixer(pa, batch, seq_len, mu_p, rwkv_vec[l], rwkv_r_k[l].reshape(1, bw), w2p, a2p, rwkv_g2[l], e_blk)
        yb = _sb_mixer(pb, batch, seq_len)
        yc = _fox_mixer(pc, batch, seq_len, fox_qk_g[l], b_f_pad)
        yd = _dn_mixer(pd, batch, seq_len, dn_conv[l], gparams, dn_norm_g[l][None, :])

        xf = _merge(xf, hn, g[3:4], (ya, yb, yc, yd), w_gate[l].astype(BF16), b_gate[l][:, None, :],
                    w_branch[l].astype(BF16), w_out[l].astype(BF16))

        xf = _ffn(xf, g[4:5], g[5:6], ffn_w_in[l, 1].astype(BF16), ffn_w_out[l, 1].astype(BF16))
    return xf.reshape(batch, seq_len, d)
```

```python
import functools

import jax
import jax.numpy as jnp
from jax import lax
from jax.experimental import pallas as pl
from jax.experimental.pallas import tpu as pltpu

F32 = jnp.float32
BF16 = jnp.bfloat16

NORM_EPS = 1e-6
RW_GN_EPS = 64e-5
L2_EPS = 1e-12

LANES = 128
SUBLANES = 8
VMEM_LIMIT = 56 * 1024 * 1024

RW_HD = 64
RW_CHUNK = 64
HD = 128
DN_CHUNK = 128
DN_CONV = 4
ATT_BLK = 256
ATT_HG = 4


def _cparams(sem):
    return pltpu.CompilerParams(dimension_semantics=sem, vmem_limit_bytes=VMEM_LIMIT)


def _dot(a, b):
    return jnp.dot(a, b, preferred_element_type=F32)


def _dot_nt(a, b):
    return lax.dot_general(a, b, (((1,), (1,)), ((), ())), preferred_element_type=F32)


def _dot_tn(a, b):
    return lax.dot_general(a, b, (((0,), (0,)), ((), ())), preferred_element_type=F32)


def _split(x):
    hi = x.astype(BF16)
    lo = (x - hi.astype(F32)).astype(BF16)
    return hi, lo


def _dot_x2(a_exact, x):
    hi, lo = _split(x)
    return _dot(a_exact, hi) + _dot(a_exact, lo)


def _dot_2x(x, b_exact):
    hi, lo = _split(x)
    return _dot(hi, b_exact) + _dot(lo, b_exact)


def _headsum(x, e_red, e_exp):
    return _dot_2x(_dot_2x(x, e_red), e_exp)


def _dot_3(a, b):
    ah, al = _split(a)
    bh, bl = _split(b)
    return _dot(ah, bh) + _dot(ah, bl) + _dot(al, bh)


def _sigmoid(x):
    return 1.0 / (1.0 + jnp.exp(-x))


def _softplus(x):
    return jnp.maximum(x, 0.0) + jnp.log(1.0 + jnp.exp(-jnp.abs(x)))


def _rms(x, g):
    return x * lax.rsqrt(jnp.mean(x * x, axis=-1, keepdims=True) + NORM_EPS) * g


def _iota2(shape, dim):
    return lax.broadcasted_iota(jnp.int32, shape, dim)


def _tri(n, strict):
    r, c = _iota2((n, n), 0), _iota2((n, n), 1)
    return (c < r) if strict else (c <= r)


def _shift_rows(x, halo, k):
    rolled = pltpu.roll(x, k, axis=0)
    hx = pltpu.roll(halo, k, axis=0)
    top = jnp.where(_iota2(hx.shape, 0) < k, hx, rolled[0:SUBLANES])
    return jnp.concatenate([top, rolled[SUBLANES:]], axis=0)


def _inv_unit_lower_multi(n_mats, steps):
    n = n_mats[0].shape[0]
    eye = jnp.where(_iota2((n, n), 0) == _iota2((n, n), 1), 1.0, 0.0)
    ps = [eye + nm for nm in n_mats]
    ms = list(n_mats)
    for _ in range(steps):
        mbs = [m.astype(BF16) for m in ms]
        ms = [_dot(mb, mb) for mb in mbs]
        ps = [p + _dot(p.astype(BF16), m.astype(BF16)) for p, m in zip(ps, ms)]
    res = [(eye - p) + _dot_3(nm, p) for nm, p in zip(n_mats, ps)]
    return [p + _dot(p.astype(BF16), r.astype(BF16)) for p, r in zip(ps, res)]


def _inv_unit_lower(n_mat, steps):
    return _inv_unit_lower_multi([n_mat], steps)[0]


def _ffn_kernel(x_ref, g0_ref, g1_ref, wg_ref, wu_ref, wo_ref, o_ref, xn_ref, acc_ref):
    j = pl.program_id(1)

    @pl.when(j == 0)
    def _():
        xn_ref[...] = _rms(x_ref[...], g0_ref[...]).astype(BF16)
        acc_ref[...] = jnp.zeros_like(acc_ref)

    xn = xn_ref[...]
    gate = _dot(xn, wg_ref[...])
    up = _dot(xn, wu_ref[...])
    act = (gate * _sigmoid(gate) * up).astype(BF16)
    acc_ref[...] += _dot(act, wo_ref[...])

    @pl.when(j == pl.num_programs(1) - 1)
    def _():
        o_ref[...] = x_ref[...] + 0.5 * _rms(acc_ref[...], g1_ref[...])


def _ffn(x, g0, g1, w_in, w_out, l, k, tm=512, tf=512):
    m, d = x.shape
    dff = w_out.shape[2]
    nf = dff // tf
    return pl.pallas_call(
        _ffn_kernel,
        out_shape=jax.ShapeDtypeStruct((m, d), F32),
        grid=(m // tm, nf),
        in_specs=[
            pl.BlockSpec((tm, d), lambda i, j: (i, 0)),
            pl.BlockSpec((1, d), lambda i, j: (0, 0)),
            pl.BlockSpec((1, d), lambda i, j: (0, 0)),
            pl.BlockSpec((None, None, d, tf), lambda i, j: (l, k, 0, j)),
            pl.BlockSpec((None, None, d, tf), lambda i, j: (l, k, 0, j + nf)),
            pl.BlockSpec((None, None, tf, d), lambda i, j: (l, k, j, 0)),
        ],
        out_specs=pl.BlockSpec((tm, d), lambda i, j: (i, 0)),
        scratch_shapes=[pltpu.VMEM((tm, d), BF16), pltpu.VMEM((tm, d), F32)],
        compiler_params=_cparams(("parallel", "arbitrary")),
        name="ffn",
    )(x, g0, g1, w_in, w_in, w_out)


def _norm_kernel(x_ref, g_ref, o_ref):
    o_ref[...] = _rms(x_ref[...], g_ref[...]).astype(BF16)


def _norm(x, g, tm=512):
    m, d = x.shape
    return pl.pallas_call(
        _norm_kernel,
        out_shape=jax.ShapeDtypeStruct((m, d), BF16),
        grid=(m // tm,),
        in_specs=[pl.BlockSpec((tm, d), lambda i: (i, 0)), pl.BlockSpec((1, d), lambda i: (0, 0))],
        out_specs=pl.BlockSpec((tm, d), lambda i: (i, 0)),
        compiler_params=_cparams(("parallel",)),
        name="norm",
    )(x, g)


def _proj_kernel(h_ref, w_ref, o_ref):
    o_ref[...] = _dot(h_ref[...], w_ref[...]).astype(o_ref.dtype)


def _proj(hn, w, tn, out_dtype, tm=1024, name="proj"):
    m, d = hn.shape
    n = w.shape[1]
    return pl.pallas_call(
        _proj_kernel,
        out_shape=jax.ShapeDtypeStruct((m, n), out_dtype),
        grid=(n // tn, m // tm),
        in_specs=[
            pl.BlockSpec((tm, d), lambda j, i: (i, 0)),
            pl.BlockSpec((d, tn), lambda j, i: (0, j)),
        ],
        out_specs=pl.BlockSpec((tm, tn), lambda j, i: (i, j)),
        compiler_params=_cparams(("parallel", "arbitrary")),
        name=name,
    )(hn, w)


RW_W = 1024
RW_PA = 3584


def _rwkv_pre_kernel(seq_len, p_ref, halo_ref, mu_ref, vec_ref, rk_ref, w2_ref, a2_ref, g2_ref, er_ref, ex_ref,
                     r_ref, lw_ref, k_ref, v_ref, nkk_ref, b_ref, g_ref, bonus_ref):
    tr = p_ref.shape[0]
    x = p_ref[...]
    first = (pl.program_id(0) * tr) % seq_len == 0
    halo = jnp.where(first, 0.0, halo_ref[...])
    prev = _shift_rows(x, halo, 1)
    x = x + (prev - x) * mu_ref[...]
    w = RW_W
    r, k, v = x[:, 0:w], x[:, w:2 * w], x[:, 2 * w:3 * w]
    w_lo, a_lo, g_lo = x[:, 3 * w:3 * w + 128], x[:, 3 * w + 128:3 * w + 256], x[:, 3 * w + 256:3 * w + 512]
    w0, a0, k_k, k_a = vec_ref[0:1, :], vec_ref[1:2, :], vec_ref[2:3, :], vec_ref[3:4, :]
    lw = -jnp.exp(-0.5) * _sigmoid(w0 + _dot_3(jnp.tanh(w_lo), w2_ref[...]))
    a = _sigmoid(a0 + _dot_3(a_lo, a2_ref[...]))
    g = _dot_3(_sigmoid(g_lo), g2_ref[...])
    e_red, e_exp = er_ref[...], ex_ref[...]
    kk = k * k_k
    kk = kk * lax.rsqrt(_headsum(kk * kk, e_red, e_exp) + L2_EPS)
    k = k * (1.0 + (a - 1.0) * k_a)
    bonus = _headsum(r * k * rk_ref[...], e_red, e_exp) * v
    r_ref[...] = r
    lw_ref[...] = lw
    k_ref[...] = k
    v_ref[...] = v
    nkk_ref[...] = -kk
    b_ref[...] = kk * a
    g_ref[...] = g
    bonus_ref[...] = bonus


def _rwkv_scan_kernel(r_ref, lw_ref, k_ref, v_ref, nkk_ref, b_ref, o_ref, s_ref):
    @pl.when(pl.program_id(1) == 0)
    def _():
        s_ref[...] = jnp.zeros_like(s_ref)

    c = RW_CHUNK
    lw = lw_ref[...]
    ltri = jnp.where(_tri(c, strict=False), 1.0, 0.0).astype(BF16)
    cum = _dot_x2(ltri, lw)
    tot = cum[c - 1:c, :]
    e_in, e_neg = jnp.exp(cum), jnp.exp(-cum)
    e_ex, e_end = jnp.exp(cum - lw), jnp.exp(tot - cum)
    r, k, v, nkk, b = r_ref[...], k_ref[...], v_ref[...], nkk_ref[...], b_ref[...]
    rt, kt, bt, at = r * e_in, k * e_neg, b * e_neg, nkk * e_ex
    kh, bh = k * e_end, b * e_end
    e_tot = jnp.exp(tot)

    n2 = 2 * c
    lane_lo = _iota2((c, LANES), 1) < RW_HD
    ri, ci = _iota2((n2, n2), 0) & (c - 1), _iota2((n2, n2), 1) & (c - 1)
    strict, incl = ci < ri, ci <= ri

    def stack(xp):
        return jnp.concatenate([jnp.where(lane_lo, xp, 0.0), jnp.where(lane_lo, 0.0, xp)], axis=0).astype(BF16)

    pairs = range(RW_W // LANES)
    sls = [slice(pr * LANES, (pr + 1) * LANES) for pr in pairs]
    at_s = [stack(at[:, sl]) for sl in sls]
    rt_s = [stack(rt[:, sl]) for sl in sls]
    bt_s = [stack(bt[:, sl]) for sl in sls]
    kt_s = [stack(kt[:, sl]) for sl in sls]
    v_s = [stack(v[:, sl]) for sl in sls]
    gram = [_dot_nt(jnp.concatenate([at_s[p], rt_s[p]], axis=0), jnp.concatenate([bt_s[p], kt_s[p]], axis=0))
            for p in pairs]
    t_inv = _inv_unit_lower_multi([jnp.where(strict, gm[:n2, :n2], 0.0) for gm in gram], 5)
    akv = [_dot(jnp.where(strict, gram[p][:n2, n2:], 0.0).astype(BF16), v_s[p]) for p in pairs]
    wu = [_dot(t_inv[p].astype(BF16), jnp.concatenate([at_s[p], akv[p].astype(BF16)], axis=1))
          for p in pairs]
    s0 = [s_ref[p] for p in pairs]
    ws = [_dot_nt(jnp.concatenate([wu[p][:, :LANES].astype(BF16), rt_s[p]], axis=0), s0[p].astype(BF16))
          for p in pairs]
    uv = [jnp.concatenate([(ws[p][:n2] + wu[p][:, LANES:]).astype(BF16), v_s[p]], axis=0) for p in pairs]
    for p in pairs:
        a_r = jnp.concatenate([jnp.where(incl, gram[p][n2:, :n2], 0.0), jnp.where(incl, gram[p][n2:, n2:], 0.0)], axis=1)
        o_s = ws[p][n2:] + _dot(a_r.astype(BF16), uv[p])
        o_ref[:, sls[p]] = o_s[:c] + o_s[c:]
    for p in pairs:
        bk = jnp.concatenate([stack(bh[:, sls[p]]), stack(kh[:, sls[p]])], axis=0)
        s_ref[p] = s0[p] * e_tot[:, sls[p]] + _dot_tn(uv[p], bk)


def _rwkv_post_kernel(o_ref, bonus_ref, g_ref, ln_ref, er_ref, ex_ref, y_ref):
    o = o_ref[...]
    e_red, e_exp = er_ref[...], ex_ref[...]
    mean = _headsum(o, e_red, e_exp) * (1.0 / RW_HD)
    d = o - mean
    var = _headsum(d * d, e_red, e_exp) * (1.0 / RW_HD)
    o = d * lax.rsqrt(var + RW_GN_EPS) * ln_ref[0:1, :] + ln_ref[1:2, :]
    y_ref[...] = ((o + bonus_ref[...]) * g_ref[...]).astype(BF16)


def _rwkv_mixer(pa, batch, seq_len, mu, vec, r_k, w2, a2, g2, e_red, e_exp, tr=256):
    m = pa.shape[0]
    w = RW_W
    row = lambda i: (i, 0)
    fix = lambda i: (0, 0)
    wide = jax.ShapeDtypeStruct((m, w), F32)
    r, lw, k, v, nkk, b, g, bonus = pl.pallas_call(
        functools.partial(_rwkv_pre_kernel, seq_len),
        out_shape=[wide] * 8,
        grid=(m // tr,),
        in_specs=[
            pl.BlockSpec((tr, RW_PA), row),
            pl.BlockSpec((SUBLANES, RW_PA), lambda i: (jnp.maximum(i * (tr // SUBLANES) - 1, 0), 0)),
            pl.BlockSpec((1, RW_PA), fix),
            pl.BlockSpec((4, w), fix),
            pl.BlockSpec((1, w), fix),
            pl.BlockSpec((LANES, w), fix),
            pl.BlockSpec((LANES, w), fix),
            pl.BlockSpec((2 * LANES, w), fix),
            pl.BlockSpec((w, LANES), fix),
            pl.BlockSpec((LANES, w), fix),
        ],
        out_specs=[pl.BlockSpec((tr, w), row)] * 8,
        compiler_params=_cparams(("parallel",)),
        name="rwkv_pre",
    )(pa, pa, mu, vec[0:4], r_k, w2, a2, g2, e_red, e_exp)

    c = RW_CHUNK
    nchunk = seq_len // c
    blk = pl.BlockSpec((c, w), lambda bi, ci: (bi * nchunk + ci, 0))
    o = pl.pallas_call(
        _rwkv_scan_kernel,
        out_shape=wide,
        grid=(batch, nchunk),
        in_specs=[blk] * 6,
        out_specs=blk,
        scratch_shapes=[pltpu.VMEM((w // LANES, LANES, LANES), F32)],
        compiler_params=_cparams(("parallel", "arbitrary")),
        name="rwkv_scan",
    )(r, lw, k, v, nkk, b)

    return pl.pallas_call(
        _rwkv_post_kernel,
        out_shape=jax.ShapeDtypeStruct((m, w), BF16),
        grid=(m // tr,),
        in_specs=[pl.BlockSpec((tr, w), row)] * 3 + [pl.BlockSpec((2, w), fix), pl.BlockSpec((w, LANES), fix),
                  pl.BlockSpec((LANES, w), fix)],
        out_specs=pl.BlockSpec((tr, w), row),
        compiler_params=_cparams(("parallel",)),
        name="rwkv_post",
    )(o, bonus, g, vec[4:6], e_red, e_exp)


def _sb_kernel(q_ref, kb_ref, vb_ref, o_ref):
    i = pl.program_id(2)
    tb = ATT_BLK
    heads = range(ATT_HG)
    hsl = [slice(hh * HD, (hh + 1) * HD) for hh in heads]
    q = [(q_ref[:, sl].astype(F32) * (HD ** -0.5)).astype(BF16) for sl in hsl]
    r, c = _iota2((tb, tb), 0), _iota2((tb, tb), 1)
    upper = jnp.where(r > c, 1.0, 0.0).astype(BF16)
    diag_ok = c < r

    def block(j, carry, acc, masked):
        rows = pl.ds(pl.multiple_of(j * tb, tb), tb)
        z = [_dot_nt(q[hh], kb_ref[rows, hsl[hh]]) for hh in heads]
        sp = [_softplus(zz) for zz in z]
        lk = [jnp.where(diag_ok, -s, 0.0) if masked else -s for s in sp]
        later = [_dot_2x(lk[hh], upper) + carry[hh] for hh in heads]
        a = [jnp.exp(z[hh] - sp[hh] + later[hh]) for hh in heads]
        if masked:
            a = [jnp.where(diag_ok, aa, 0.0) for aa in a]
        acc = [acc[hh] + _dot(a[hh].astype(BF16), vb_ref[rows, hsl[hh]]) for hh in heads]
        carry = [later[hh][:, 0:1] + lk[hh][:, 0:1] for hh in heads]
        return carry, acc

    st = block(i, [jnp.zeros((tb, 1), F32)] * ATT_HG, [jnp.zeros((tb, HD), F32)] * ATT_HG, True)

    def body(n, st):
        return block(i - 1 - n, st[0], st[1], False)

    carry, acc = lax.fori_loop(0, i, body, st)
    for hh in heads:
        o_ref[:, hsl[hh]] = acc[hh].astype(BF16)


def _sb_mixer(pb, batch, seq_len):
    m = pb.shape[0]
    tb = ATT_BLK
    nq = seq_len // tb
    ng = 1024 // (HD * ATT_HG)
    gw = HD * ATT_HG
    return pl.pallas_call(
        _sb_kernel,
        out_shape=jax.ShapeDtypeStruct((m, 1024), BF16),
        grid=(batch, ng, nq),
        in_specs=[
            pl.BlockSpec((tb, gw), lambda b, h, i: (b * nq + i, h)),
            pl.BlockSpec((seq_len, gw), lambda b, h, i: (b, ng + h)),
            pl.BlockSpec((seq_len, gw), lambda b, h, i: (b, 2 * ng + h)),
        ],
        out_specs=pl.BlockSpec((tb, gw), lambda b, h, i: (b * nq + i, h)),
        compiler_params=_cparams(("parallel", "parallel", "arbitrary")),
        name="stick_breaking",
    )(pb, pb, pb)


def _fox_cum_kernel(f_ref, bf_ref, cum_ref, cumt_ref):
    tb = LANES
    x = f_ref[...] + bf_ref[...]
    lf = jnp.minimum(x, 0.0) - jnp.log(1.0 + jnp.exp(-jnp.abs(x)))
    ltri = jnp.where(_tri(tb, strict=False), 1.0, 0.0).astype(BF16)
    carry = jnp.zeros((1, LANES), F32)
    for blk in range(f_ref.shape[0] // tb):
        xb = lf[blk * tb:(blk + 1) * tb, :]
        hi = xb.astype(BF16)
        mid = (xb - hi.astype(F32)).astype(BF16)
        lo = (xb - hi.astype(F32) - mid.astype(F32)).astype(BF16)
        cb = _dot(ltri, hi) + _dot(ltri, mid) + _dot(ltri, lo) + carry
        cum_ref[blk * tb:(blk + 1) * tb, :] = cb
        cumt_ref[:, blk * tb:(blk + 1) * tb] = cb.T
        carry = cb[tb - 1:tb, :]


def _fox_kernel(q_ref, k_ref, vb_ref, g_ref, cq_ref, ck_ref, o_ref, kn_ref):
    hg = pl.program_id(1)
    i = pl.program_id(2)
    tb = ATT_BLK
    heads = range(ATT_HG)
    hsl = [slice(hh * HD, (hh + 1) * HD) for hh in heads]

    def hnorm(x, g):
        x = x.astype(F32)
        return x * lax.rsqrt(jnp.mean(x * x, axis=-1, keepdims=True) + NORM_EPS) * g

    @pl.when(i == 0)
    def _():
        for sl in hsl:
            kn_ref[:, sl] = hnorm(k_ref[:, sl], g_ref[1:2, :]).astype(BF16)

    q = [(hnorm(q_ref[:, sl], g_ref[0:1, :]) * (HD ** -0.5)).astype(BF16) for sl in hsl]
    cq_all = cq_ref[...]
    lane = _iota2((tb, LANES), 1)
    cq = [jnp.sum(jnp.where(lane == hg * ATT_HG + hh, cq_all, 0.0), axis=-1, keepdims=True) for hh in heads]
    r, c = _iota2((tb, tb), 0), _iota2((tb, tb), 1)
    causal = c <= r

    def block(j, m_run, l_run, acc, masked):
        rows = pl.ds(pl.multiple_of(j * tb, tb), tb)
        ck8 = ck_ref[:, rows]
        sub = _iota2(ck8.shape, 0)
        ck = [jnp.sum(jnp.where(sub == hg * ATT_HG + hh, ck8, 0.0), axis=0, keepdims=True) for hh in heads]
        s = [_dot_nt(q[hh], kn_ref[rows, hsl[hh]]) + (cq[hh] - ck[hh]) for hh in heads]
        if masked:
            s = [jnp.where(causal, ss, -jnp.inf) for ss in s]
        m_new = [jnp.maximum(m_run[hh], jnp.max(s[hh], axis=-1, keepdims=True)) for hh in heads]
        alpha = [jnp.exp(m_run[hh] - m_new[hh]) for hh in heads]
        p = [jnp.exp(s[hh] - m_new[hh]) for hh in heads]
        l_new = [alpha[hh] * l_run[hh] + jnp.sum(p[hh], axis=-1, keepdims=True) for hh in heads]
        acc = [alpha[hh] * acc[hh] + _dot(p[hh].astype(BF16), vb_ref[rows, hsl[hh]]) for hh in heads]
        return m_new, l_new, acc

    st = block(i, [jnp.full((tb, 1), -jnp.inf, F32)] * ATT_HG, [jnp.zeros((tb, 1), F32)] * ATT_HG,
               [jnp.zeros((tb, HD), F32)] * ATT_HG, True)

    def body(n, st):
        return block(i - 1 - n, st[0], st[1], st[2], False)

    m_run, l_run, acc = lax.fori_loop(0, i, body, st)
    for hh in heads:
        o_ref[:, hsl[hh]] = (acc[hh] / l_run[hh]).astype(BF16)


def _fox_mixer(pc, ps, batch, seq_len, qk_g, b_f_pad):
    m = pc.shape[0]
    tb = ATT_BLK
    nq = seq_len // tb
    cum, cumt = pl.pallas_call(
        _fox_cum_kernel,
        out_shape=[jax.ShapeDtypeStruct((m, LANES), F32), jax.ShapeDtypeStruct((batch * LANES, seq_len), F32)],
        grid=(batch,),
        in_specs=[pl.BlockSpec((seq_len, LANES), lambda b: (b, 0)), pl.BlockSpec((1, LANES), lambda b: (0, 0))],
        out_specs=[pl.BlockSpec((seq_len, LANES), lambda b: (b, 0)), pl.BlockSpec((LANES, seq_len), lambda b: (b, 0))],
        compiler_params=_cparams(("parallel",)),
        name="fox_cum",
    )(ps, b_f_pad)
    ng = 1024 // (HD * ATT_HG)
    gw = HD * ATT_HG
    return pl.pallas_call(
        _fox_kernel,
        out_shape=jax.ShapeDtypeStruct((m, 1024), BF16),
        grid=(batch, ng, nq),
        in_specs=[
            pl.BlockSpec((tb, gw), lambda b, h, i: (b * nq + i, h)),
            pl.BlockSpec((seq_len, gw), lambda b, h, i: (b, ng + h)),
            pl.BlockSpec((seq_len, gw), lambda b, h, i: (b, 2 * ng + h)),
            pl.BlockSpec((2, HD), lambda b, h, i: (0, 0)),
            pl.BlockSpec((tb, LANES), lambda b, h, i: (b * nq + i, 0)),
            pl.BlockSpec((SUBLANES, seq_len), lambda b, h, i: (b * (LANES // SUBLANES), 0)),
        ],
        out_specs=pl.BlockSpec((tb, gw), lambda b, h, i: (b * nq + i, h)),
        scratch_shapes=[pltpu.VMEM((seq_len, gw), BF16)],
        compiler_params=_cparams(("parallel", "parallel", "arbitrary")),
        name="forgetting_attention",
    )(pc, pc, pc, qk_g, cum, cumt)


HALO_BF16 = 16


def _dn_pre_kernel(seq_len, p_ref, halo_ref, s_ref, cw_ref, gp_ref, q_ref, k_ref, v_ref, gb_ref):
    tr = p_ref.shape[0]
    w = 1024
    first = (pl.program_id(0) * tr) % seq_len == 0
    x = p_ref[...].astype(F32)
    halo = jnp.where(first, 0.0, halo_ref[HALO_BF16 - SUBLANES:, :].astype(F32))
    acc = x * cw_ref[DN_CONV - 1:DN_CONV, :]
    for s in range(1, DN_CONV):
        acc = acc + _shift_rows(x, halo, s) * cw_ref[DN_CONV - 1 - s:DN_CONV - s, :]
    y = acc * _sigmoid(acc)
    for hh in range(w // HD):
        qs = y[:, hh * HD:(hh + 1) * HD]
        ks = y[:, w + hh * HD:w + (hh + 1) * HD]
        q_ref[:, hh * HD:(hh + 1) * HD] = qs * lax.rsqrt(jnp.sum(qs * qs, -1, keepdims=True) + L2_EPS) * (HD ** -0.5)
        k_ref[:, hh * HD:(hh + 1) * HD] = ks * lax.rsqrt(jnp.sum(ks * ks, -1, keepdims=True) + L2_EPS)
    v_ref[...] = y[:, 2 * w:3 * w]
    s = s_ref[...]
    nh = w // HD
    beta = _sigmoid(s)
    g = -jnp.exp(gp_ref[0:1, :]) * _softplus(s + gp_ref[1:2, :])
    gb_ref[...] = jnp.where(_iota2(s.shape, 1) < nh, beta, g)


def _dn_scan_kernel(q_ref, k_ref, v_ref, z_ref, gb_ref, ng_ref, y_ref, s_ref):
    @pl.when(pl.program_id(1) == 0)
    def _():
        s_ref[...] = jnp.zeros_like(s_ref)

    c = DN_CHUNK
    nh = 1024 // HD
    gb = gb_ref[...]
    ltri = jnp.where(_tri(c, strict=False), 1.0, 0.0).astype(BF16)
    hi = gb.astype(BF16)
    mid = (gb - hi.astype(F32)).astype(BF16)
    lo = (gb - hi.astype(F32) - mid.astype(F32)).astype(BF16)
    gc = _dot(ltri, hi) + _dot(ltri, mid) + _dot(ltri, lo)
    gct = gc.T
    strict, incl = _tri(c, strict=True), _tri(c, strict=False)
    heads = range(nh)
    sls = [slice(h * HD, (h + 1) * HD) for h in heads]
    q = [q_ref[:, sl] for sl in sls]
    k = [k_ref[:, sl] for sl in sls]
    beta = [gb[:, h:h + 1] for h in heads]
    gcol = [gc[:, nh + h:nh + h + 1] for h in heads]
    glast = [gc[c - 1:c, nh + h:nh + h + 1] for h in heads]
    eg = [jnp.exp(gcol[h]) for h in heads]
    decay = [jnp.exp(jnp.minimum(gcol[h] - gct[nh + h:nh + h + 1, :], 0.0)) for h in heads]
    kb = [k[h] * beta[h] for h in heads]
    gram = [_dot_nt(jnp.concatenate([kb[h].astype(BF16), q[h].astype(BF16)], axis=0), k[h].astype(BF16))
            for h in heads]
    t_inv = _inv_unit_lower_multi([jnp.where(strict, -gram[h][:c] * decay[h], 0.0) for h in heads], 6)
    uw = [_dot(t_inv[h].astype(BF16),
               jnp.concatenate([(v_ref[:, sls[h]] * beta[h]).astype(BF16), (kb[h] * eg[h]).astype(BF16)], axis=1))
          for h in heads]
    s0 = [s_ref[h] for h in heads]
    ws = [_dot(jnp.concatenate([uw[h][:, HD:].astype(BF16), (q[h] * eg[h]).astype(BF16)], axis=0), s0[h].astype(BF16))
          for h in heads]
    vnb = [(uw[h][:, :HD] - ws[h][:c]).astype(BF16) for h in heads]
    for h in heads:
        attn = jnp.where(incl, gram[h][c:] * decay[h], 0.0)
        o = ws[h][c:] + _dot(attn.astype(BF16), vnb[h])
        o = o * lax.rsqrt(jnp.mean(o * o, axis=-1, keepdims=True) + NORM_EPS) * ng_ref[...]
        z = z_ref[:, sls[h]].astype(F32)
        y_ref[:, sls[h]] = (o * (z * _sigmoid(z))).astype(BF16)
    for h in heads:
        s_ref[h] = s0[h] * jnp.exp(glast[h]) + _dot_tn((k[h] * jnp.exp(glast[h] - gcol[h])).astype(BF16), vnb[h])


def _dn_mixer(pd, ps, batch, seq_len, conv_w, gparams, norm_g, tr=256):
    m = pd.shape[0]
    w = 1024
    row = lambda i: (i, 0)
    fix = lambda i: (0, 0)
    wide = jax.ShapeDtypeStruct((m, w), F32)
    q, k, v, gb = pl.pallas_call(
        functools.partial(_dn_pre_kernel, seq_len),
        out_shape=[wide, wide, wide, jax.ShapeDtypeStruct((m, LANES), F32)],
        grid=(m // tr,),
        in_specs=[
            pl.BlockSpec((tr, 3 * w), row),
            pl.BlockSpec((HALO_BF16, 3 * w), lambda i: (jnp.maximum(i * (tr // HALO_BF16) - 1, 0), 0)),
            pl.BlockSpec((tr, LANES), lambda i: (i, 1)),
            pl.BlockSpec((DN_CONV, 3 * w), fix),
            pl.BlockSpec((2, LANES), fix),
        ],
        out_specs=[pl.BlockSpec((tr, w), row)] * 3 + [pl.BlockSpec((tr, LANES), row)],
        compiler_params=_cparams(("parallel",)),
        name="deltanet_pre",
    )(pd, pd, ps, conv_w, gparams)

    c = DN_CHUNK
    nchunk = seq_len // c
    blk = pl.BlockSpec((c, w), lambda bi, ci: (bi * nchunk + ci, 0))
    return pl.pallas_call(
        _dn_scan_kernel,
        out_shape=jax.ShapeDtypeStruct((m, w), BF16),
        grid=(batch, nchunk),
        in_specs=[blk, blk, blk,
                  pl.BlockSpec((c, w), lambda bi, ci: (bi * nchunk + ci, 3)),
                  pl.BlockSpec((c, LANES), lambda bi, ci: (bi * nchunk + ci, 0)),
                  pl.BlockSpec((1, HD), lambda bi, ci: (0, 0))],
        out_specs=blk,
        scratch_shapes=[pltpu.VMEM((w // HD, HD, HD), F32)],
        compiler_params=_cparams(("parallel", "arbitrary")),
        name="deltanet_scan",
    )(q, k, v, pd, gb, norm_g)


def _merge_kernel(hn_ref, ya_ref, yb_ref, yc_ref, yd_ref, wg_ref, bg_ref, wb_ref, o_ref, acc_ref):
    i = pl.program_id(1)

    @pl.when(i == 0)
    def _():
        acc_ref[...] = jnp.zeros_like(acc_ref)

    gate = _sigmoid(_dot(hn_ref[...], wg_ref[...]) + bg_ref[...])
    ys = (ya_ref, yb_ref, yc_ref, yd_ref)
    for n in range(4):
        @pl.when(i == n)
        def _(n=n):
            acc_ref[...] += gate * _dot(ys[n][...], wb_ref[...])

    @pl.when(i == 3)
    def _():
        o_ref[...] = acc_ref[...].astype(BF16)


def _outproj_kernel(x_ref, m_ref, w_ref, g_ref, o_ref):
    o_ref[...] = x_ref[...] + _rms(_dot(m_ref[...], w_ref[...]), g_ref[...])


def _merge(x, hn, g3, ys, w_gate, b_gate, w_branch, w_out, l, tm=512):
    m, d = x.shape
    bw = ys[0].shape[1]
    row = lambda r, i: (r, 0)
    merged = pl.pallas_call(
        _merge_kernel,
        out_shape=jax.ShapeDtypeStruct((m, d), BF16),
        grid=(m // tm, 4),
        in_specs=[
            pl.BlockSpec((tm, d), row),
            pl.BlockSpec((tm, bw), row),
            pl.BlockSpec((tm, bw), row),
            pl.BlockSpec((tm, bw), row),
            pl.BlockSpec((tm, bw), row),
            pl.BlockSpec((None, None, d, d), lambda r, i: (l, i, 0, 0)),
            pl.BlockSpec((None, None, 1, d), lambda r, i: (l, i, 0, 0)),
            pl.BlockSpec((None, None, bw, d), lambda r, i: (l, i, 0, 0)),
        ],
        out_specs=pl.BlockSpec((tm, d), row),
        scratch_shapes=[pltpu.VMEM((tm, d), F32)],
        compiler_params=_cparams(("parallel", "arbitrary")),
        name="merge",
    )(hn, ys[0], ys[1], ys[2], ys[3], w_gate, b_gate, w_branch)
    return pl.pallas_call(
        _outproj_kernel,
        out_shape=jax.ShapeDtypeStruct((m, d), F32),
        grid=(m // tm,),
        in_specs=[
            pl.BlockSpec((tm, d), lambda r: (r, 0)),
            pl.BlockSpec((tm, d), lambda r: (r, 0)),
            pl.BlockSpec((None, d, d), lambda r: (l, 0, 0)),
            pl.BlockSpec((1, d), lambda r: (0, 0)),
        ],
        out_specs=pl.BlockSpec((tm, d), lambda r: (r, 0)),
        compiler_params=_cparams(("parallel",)),
        name="outproj",
    )(x, merged, w_out, g3)


def _pad_cols(w, n):
    return jnp.pad(w, ((0, 0), (0, n - w.shape[1])))


def _pad_rows(w, n):
    return jnp.pad(w, ((0, n - w.shape[0]), (0, 0)))


def kernel(x, norm_g, ffn_w_in, ffn_w_out, w_in, w_gate, b_gate, w_branch, w_out, rwkv_mu, rwkv_vec, rwkv_r_k,
           rwkv_w2, rwkv_a2, rwkv_g2, fox_qk_g, fox_b_f, dn_conv, dn_A_log, dn_dt_bias, dn_norm_g):
    batch, seq_len, d = x.shape
    depth = norm_g.shape[0]
    bw = w_branch.shape[2]
    rw_lo = rwkv_w2.shape[1]
    ra_lo = rwkv_a2.shape[1]
    rg_lo = rwkv_g2.shape[1]
    nh = bw // HD
    assert bw == 1024 and rw_lo <= LANES and ra_lo <= LANES and rg_lo == 2 * LANES
    assert seq_len % DN_CHUNK == 0 and (batch * seq_len) % 512 == 0

    rw_cols = 3 * bw + rw_lo + ra_lo + rg_lo
    sb_cols = 3 * bw
    fx_cols = 3 * bw + nh
    o_a, o_b, o_c, o_d = 0, rw_cols, rw_cols + sb_cols, rw_cols + sb_cols + fx_cols

    head_of = jnp.arange(bw) // RW_HD
    e_red = (head_of[:, None] == jnp.arange(LANES)[None, :]).astype(BF16)
    e_exp = e_red.T

    w_in_b = w_in.astype(BF16)
    ffn_w_in_b, ffn_w_out_b = ffn_w_in.astype(BF16), ffn_w_out.astype(BF16)
    w_gate_b, w_branch_b, w_out_b = w_gate.astype(BF16), w_branch.astype(BF16), w_out.astype(BF16)

    xf = x.reshape(batch * seq_len, d)
    for l in range(depth):
        g = norm_g[l]
        wl = w_in_b[l]
        wa = jnp.concatenate([
            wl[:, o_a:o_a + 3 * bw],
            _pad_cols(wl[:, o_a + 3 * bw:o_a + 3 * bw + rw_lo], LANES),
            _pad_cols(wl[:, o_a + 3 * bw + rw_lo:o_a + 3 * bw + rw_lo + ra_lo], LANES),
            wl[:, o_a + 3 * bw + rw_lo + ra_lo:o_b]], axis=1)
        wb = wl[:, o_b:o_c]
        wc = wl[:, o_c:o_c + 3 * bw]
        wd = wl[:, o_d:o_d + 4 * bw]
        ws = jnp.concatenate([_pad_cols(wl[:, o_c + 3 * bw:o_d], LANES), _pad_cols(wl[:, o_d + 4 * bw:], LANES)], axis=1)
        mu = rwkv_mu[l]
        mu_p = jnp.concatenate([
            mu[0:3 * bw], jnp.pad(mu[3 * bw:3 * bw + rw_lo], (0, LANES - rw_lo)),
            jnp.pad(mu[3 * bw + rw_lo:3 * bw + rw_lo + ra_lo], (0, LANES - ra_lo)),
            mu[3 * bw + rw_lo + ra_lo:]])[None, :]
        w2p = _pad_rows(rwkv_w2[l], LANES)
        a2p = _pad_rows(rwkv_a2[l], LANES)
        b_f_pad = jnp.pad(fox_b_f[l], (0, LANES - nh))[None, :]
        gparams = jnp.stack([jnp.pad(dn_A_log[l], (nh, LANES - 2 * nh)), jnp.pad(dn_dt_bias[l], (nh, LANES - 2 * nh))])

        xf = _ffn(xf, g[0:1], g[1:2], ffn_w_in_b, ffn_w_out_b, l, 0)

        hn = _norm(xf, g[2:3])
        pa = _proj(hn, wa, 896, F32, name="proj_rwkv")
        pb = _proj(hn, wb, 1024, BF16, name="proj_sb")
        pc = _proj(hn, wc, 1024, BF16, name="proj_fox")
        pd = _proj(hn, wd, 1024, BF16, name="proj_dn")
        ps = _proj(hn, ws, 2 * LANES, F32, name="proj_gates")

        ya = _rwkv_mixer(pa, batch, seq_len, mu_p, rwkv_vec[l], rwkv_r_k[l].reshape(1, bw), w2p, a2p, rwkv_g2[l], e_red, e_exp)
        yb = _sb_mixer(pb, batch, seq_len)
        yc = _fox_mixer(pc, ps, batch, seq_len, fox_qk_g[l], b_f_pad)
        yd = _dn_mixer(pd, ps, batch, seq_len, dn_conv[l], gparams, dn_norm_g[l][None, :])

        xf = _merge(xf, hn, g[3:4], (ya, yb, yc, yd), w_gate_b, b_gate[:, :, None, :], w_branch_b, w_out_b, l)

        xf = _ffn(xf, g[4:5], g[5:6], ffn_w_in_b, ffn_w_out_b, l, 1)
    return xf.reshape(batch, seq_len, d)
```

```python
import functools

import jax
import jax.numpy as jnp
from jax import lax
from jax.experimental import pallas as pl
from jax.experimental.pallas import tpu as pltpu

F32 = jnp.float32
BF16 = jnp.bfloat16

NORM_EPS = 1e-6
RW_GN_EPS = 64e-5
L2_EPS = 1e-12
LOG2E = 1.4426950408889634

LANES = 128
SUBLANES = 8
VMEM_LIMIT = 56 * 1024 * 1024

RW_HD = 64
RW_CHUNK = 64
HD = 128
DN_CHUNK = 128
DN_CONV = 4
ATT_BLK = 256
ATT_HG = 4


def _cparams(sem):
    return pltpu.CompilerParams(dimension_semantics=sem, vmem_limit_bytes=VMEM_LIMIT)


def _dot(a, b):
    return jnp.dot(a, b, preferred_element_type=F32)


def _dot_nt(a, b):
    return lax.dot_general(a, b, (((1,), (1,)), ((), ())), preferred_element_type=F32)


def _dot_tn(a, b):
    return lax.dot_general(a, b, (((0,), (0,)), ((), ())), preferred_element_type=F32)


def _split(x):
    hi = x.astype(BF16)
    lo = (x - hi.astype(F32)).astype(BF16)
    return hi, lo


def _dot_x2(a_exact, x):
    hi, lo = _split(x)
    return _dot(a_exact, hi) + _dot(a_exact, lo)


def _dot_2x(x, b_exact):
    hi, lo = _split(x)
    return _dot(hi, b_exact) + _dot(lo, b_exact)


def _headsum(x, e_red, e_exp):
    return _dot_2x(_dot_2x(x, e_red), e_exp)


def _dot_3(a, b):
    ah, al = _split(a)
    bh, bl = _split(b)
    return _dot(ah, bh) + _dot(ah, bl) + _dot(al, bh)


def _sigmoid(x):
    return 1.0 / (1.0 + jnp.exp(-x))


def _softplus(x):
    return jnp.maximum(x, 0.0) + jnp.log(1.0 + jnp.exp(-jnp.abs(x)))


def _rms(x, g):
    return x * lax.rsqrt(jnp.mean(x * x, axis=-1, keepdims=True) + NORM_EPS) * g


def _iota2(shape, dim):
    return lax.broadcasted_iota(jnp.int32, shape, dim)


def _tri(n, strict):
    r, c = _iota2((n, n), 0), _iota2((n, n), 1)
    return (c < r) if strict else (c <= r)


def _shift_rows(x, halo, k):
    rolled = pltpu.roll(x, k, axis=0)
    hx = pltpu.roll(halo, k, axis=0)
    top = jnp.where(_iota2(hx.shape, 0) < k, hx, rolled[0:SUBLANES])
    return jnp.concatenate([top, rolled[SUBLANES:]], axis=0)


def _inv_unit_lower_multi(n_mats, steps):
    n = n_mats[0].shape[0]
    eye = jnp.where(_iota2((n, n), 0) == _iota2((n, n), 1), 1.0, 0.0)
    ps = [eye + nm for nm in n_mats]
    ms = list(n_mats)
    for _ in range(steps):
        mbs = [m.astype(BF16) for m in ms]
        ms = [_dot(mb, mb) for mb in mbs]
        ps = [p + _dot(p.astype(BF16), m.astype(BF16)) for p, m in zip(ps, ms)]
    res = [(eye - p) + _dot_3(nm, p) for nm, p in zip(n_mats, ps)]
    return [p + _dot(p.astype(BF16), r.astype(BF16)) for p, r in zip(ps, res)]


def _inv_unit_lower(n_mat, steps):
    return _inv_unit_lower_multi([n_mat], steps)[0]


def _ffn_kernel(x_ref, g0_ref, g1_ref, wg_ref, wu_ref, wo_ref, o_ref, xn_ref, acc_ref):
    j = pl.program_id(1)

    @pl.when(j == 0)
    def _():
        xn_ref[...] = _rms(x_ref[...], g0_ref[...]).astype(BF16)
        acc_ref[...] = jnp.zeros_like(acc_ref)

    xn = xn_ref[...]
    gate = _dot(xn, wg_ref[...])
    up = _dot(xn, wu_ref[...])
    act = (gate * _sigmoid(gate) * up).astype(BF16)
    acc_ref[...] += _dot(act, wo_ref[...])

    @pl.when(j == pl.num_programs(1) - 1)
    def _():
        o_ref[...] = x_ref[...] + 0.5 * _rms(acc_ref[...], g1_ref[...])


def _ffn(x, g0, g1, w_in, w_out, l, k, tm=512, tf=512):
    m, d = x.shape
    dff = w_out.shape[2]
    nf = dff // tf
    return pl.pallas_call(
        _ffn_kernel,
        out_shape=jax.ShapeDtypeStruct((m, d), F32),
        grid=(m // tm, nf),
        in_specs=[
            pl.BlockSpec((tm, d), lambda i, j: (i, 0)),
            pl.BlockSpec((1, d), lambda i, j: (0, 0)),
            pl.BlockSpec((1, d), lambda i, j: (0, 0)),
            pl.BlockSpec((None, None, d, tf), lambda i, j: (l, k, 0, j)),
            pl.BlockSpec((None, None, d, tf), lambda i, j: (l, k, 0, j + nf)),
            pl.BlockSpec((None, None, tf, d), lambda i, j: (l, k, j, 0)),
        ],
        out_specs=pl.BlockSpec((tm, d), lambda i, j: (i, 0)),
        scratch_shapes=[pltpu.VMEM((tm, d), BF16), pltpu.VMEM((tm, d), F32)],
        compiler_params=_cparams(("parallel", "arbitrary")),
        name="ffn",
    )(x, g0, g1, w_in, w_in, w_out)


def _norm_kernel(x_ref, g_ref, o_ref):
    o_ref[...] = _rms(x_ref[...], g_ref[...]).astype(BF16)


def _norm(x, g, tm=512):
    m, d = x.shape
    return pl.pallas_call(
        _norm_kernel,
        out_shape=jax.ShapeDtypeStruct((m, d), BF16),
        grid=(m // tm,),
        in_specs=[pl.BlockSpec((tm, d), lambda i: (i, 0)), pl.BlockSpec((1, d), lambda i: (0, 0))],
        out_specs=pl.BlockSpec((tm, d), lambda i: (i, 0)),
        compiler_params=_cparams(("parallel",)),
        name="norm",
    )(x, g)


def _proj_kernel(shift, width, two_blocks, h_ref, *refs):
    if two_blocks:
        w0_ref, w1_ref, o_ref, ws_ref = refs
    else:
        w0_ref, o_ref, ws_ref = refs
    j, i = pl.program_id(0), pl.program_id(1)
    tn = o_ref.shape[1]

    @pl.when(i == 0)
    def _():
        w = w0_ref[...]
        if two_blocks:
            w = jnp.concatenate([w, w1_ref[...]], axis=1)
        u = pltpu.bitcast(w, jnp.uint32)
        if shift:
            u = pltpu.roll(u, u.shape[1] - shift, axis=1)
        u = u[:, :tn]
        if width % tn:
            u = jnp.where(j * tn + _iota2(u.shape, 1) < width, u, jnp.uint32(0))
        ws_ref[...] = pltpu.bitcast(u, BF16)

    o_ref[...] = _dot(h_ref[...], ws_ref[...]).astype(o_ref.dtype)


def _proj(hn, w_all, l, off, width, tn, out_dtype, tm=1024, name="proj"):
    m, d = hn.shape
    c0, shift = off // tn, off % tn
    nt = -(-width // tn)
    two_blocks = shift > 0 and shift + min(width, tn) > tn
    w_specs = [pl.BlockSpec((None, d, tn), lambda j, i: (l, 0, c0 + j))]
    if two_blocks:
        w_specs.append(pl.BlockSpec((None, d, tn), lambda j, i: (l, 0, c0 + j + 1)))
    return pl.pallas_call(
        functools.partial(_proj_kernel, shift, width, two_blocks),
        out_shape=jax.ShapeDtypeStruct((m, nt * tn), out_dtype),
        grid=(nt, m // tm),
        in_specs=[pl.BlockSpec((tm, d), lambda j, i: (i, 0))] + w_specs,
        out_specs=pl.BlockSpec((tm, tn), lambda j, i: (i, j)),
        scratch_shapes=[pltpu.VMEM((d, tn), BF16)],
        compiler_params=_cparams(("parallel", "arbitrary")),
        name=name,
    )(hn, *([w_all] * len(w_specs)))


RW_W = 1024
RW_PA = 3584


def _rwkv_pre_kernel(seq_len, rw_lo, ra_lo, p_ref, halo_ref, mu_ref, vec_ref, rk_ref, w2_ref, a2_ref, g2_ref,
                     er_ref, ex_ref, r_ref, lw_ref, k_ref, v_ref, nkk_ref, b_ref, g_ref, bonus_ref):
    tr = p_ref.shape[0]
    x = p_ref[...]
    first = (pl.program_id(0) * tr) % seq_len == 0
    halo = jnp.where(first, 0.0, halo_ref[...])
    prev = _shift_rows(x, halo, 1)
    x = x + (prev - x) * mu_ref[...]
    w = RW_W
    r, k, v = x[:, 0:w], x[:, w:2 * w], x[:, 2 * w:3 * w]
    tail = x[:, 3 * w:]
    lane = _iota2((tr, LANES), 1)
    w_lo = jnp.where(lane < rw_lo, tail[:, 0:LANES], 0.0)
    a_lo = jnp.where(lane < ra_lo, pltpu.roll(tail[:, 0:2 * LANES], 2 * LANES - rw_lo, axis=1)[:, 0:LANES], 0.0)
    g_off = rw_lo + ra_lo - LANES
    g_lo = pltpu.roll(tail[:, LANES:], 3 * LANES - g_off, axis=1)[:, 0:2 * LANES]
    w0, a0, k_k, k_a = vec_ref[0:1, :], vec_ref[1:2, :], vec_ref[2:3, :], vec_ref[3:4, :]
    lw = -jnp.exp(-0.5) * _sigmoid(w0 + _dot_3(jnp.tanh(w_lo), w2_ref[...]))
    a = _sigmoid(a0 + _dot_3(a_lo, a2_ref[...]))
    g = _dot_3(_sigmoid(g_lo), g2_ref[...])
    e_red, e_exp = er_ref[...], ex_ref[...]
    kk = k * k_k
    kk = kk * lax.rsqrt(_headsum(kk * kk, e_red, e_exp) + L2_EPS)
    k = k * (1.0 + (a - 1.0) * k_a)
    bonus = _headsum(r * k * rk_ref[...], e_red, e_exp) * v
    r_ref[...] = r
    lw_ref[...] = lw
    k_ref[...] = k
    v_ref[...] = v
    nkk_ref[...] = -kk
    b_ref[...] = kk * a
    g_ref[...] = g
    bonus_ref[...] = bonus


def _rwkv_scan_kernel(r_ref, lw_ref, k_ref, v_ref, nkk_ref, b_ref, o_ref, s_ref):
    @pl.when(pl.program_id(1) == 0)
    def _():
        s_ref[...] = jnp.zeros_like(s_ref)

    c = RW_CHUNK
    lw = lw_ref[...]
    ltri = jnp.where(_tri(c, strict=False), 1.0, 0.0).astype(BF16)
    cum = _dot_x2(ltri, lw)
    tot = cum[c - 1:c, :]
    e_in, e_neg = jnp.exp(cum), jnp.exp(-cum)
    e_ex, e_end = jnp.exp(cum - lw), jnp.exp(tot - cum)
    r, k, v, nkk, b = r_ref[...], k_ref[...], v_ref[...], nkk_ref[...], b_ref[...]
    rt, kt, bt, at = r * e_in, k * e_neg, b * e_neg, nkk * e_ex
    kh, bh = k * e_end, b * e_end
    e_tot = jnp.exp(tot)

    n2 = 2 * c
    lane_lo = _iota2((c, LANES), 1) < RW_HD
    ri, ci = _iota2((n2, n2), 0) & (c - 1), _iota2((n2, n2), 1) & (c - 1)
    strict, incl = ci < ri, ci <= ri

    def stack(xp):
        return jnp.concatenate([jnp.where(lane_lo, xp, 0.0), jnp.where(lane_lo, 0.0, xp)], axis=0).astype(BF16)

    pairs = range(RW_W // LANES)
    sls = [slice(pr * LANES, (pr + 1) * LANES) for pr in pairs]
    at_s = [stack(at[:, sl]) for sl in sls]
    rt_s = [stack(rt[:, sl]) for sl in sls]
    bt_s = [stack(bt[:, sl]) for sl in sls]
    kt_s = [stack(kt[:, sl]) for sl in sls]
    v_s = [stack(v[:, sl]) for sl in sls]
    gram = [_dot_nt(jnp.concatenate([at_s[p], rt_s[p]], axis=0), jnp.concatenate([bt_s[p], kt_s[p]], axis=0))
            for p in pairs]
    t_inv = _inv_unit_lower_multi([jnp.where(strict, gm[:n2, :n2], 0.0) for gm in gram], 5)
    akv = [_dot(jnp.where(strict, gram[p][:n2, n2:], 0.0).astype(BF16), v_s[p]) for p in pairs]
    wu = [_dot(t_inv[p].astype(BF16), jnp.concatenate([at_s[p], akv[p].astype(BF16)], axis=1))
          for p in pairs]
    s0 = [s_ref[p] for p in pairs]
    ws = [_dot_nt(jnp.concatenate([wu[p][:, :LANES].astype(BF16), rt_s[p]], axis=0), s0[p].astype(BF16))
          for p in pairs]
    uv = [jnp.concatenate([(ws[p][:n2] + wu[p][:, LANES:]).astype(BF16), v_s[p]], axis=0) for p in pairs]
    for p in pairs:
        a_r = jnp.concatenate([jnp.where(incl, gram[p][n2:, :n2], 0.0), jnp.where(incl, gram[p][n2:, n2:], 0.0)], axis=1)
        o_s = ws[p][n2:] + _dot(a_r.astype(BF16), uv[p])
        o_ref[:, sls[p]] = o_s[:c] + o_s[c:]
    for p in pairs:
        bk = jnp.concatenate([stack(bh[:, sls[p]]), stack(kh[:, sls[p]])], axis=0)
        s_ref[p] = s0[p] * e_tot[:, sls[p]] + _dot_tn(uv[p], bk)


def _rwkv_post_kernel(o_ref, bonus_ref, g_ref, ln_ref, er_ref, ex_ref, y_ref):
    o = o_ref[...]
    e_red, e_exp = er_ref[...], ex_ref[...]
    mean = _headsum(o, e_red, e_exp) * (1.0 / RW_HD)
    d = o - mean
    var = _headsum(d * d, e_red, e_exp) * (1.0 / RW_HD)
    o = d * lax.rsqrt(var + RW_GN_EPS) * ln_ref[0:1, :] + ln_ref[1:2, :]
    y_ref[...] = ((o + bonus_ref[...]) * g_ref[...]).astype(BF16)


def _rwkv_mixer(pa, batch, seq_len, rw_lo, ra_lo, mu, vec, r_k, w2, a2, g2, e_red, e_exp, tr=256):
    m = pa.shape[0]
    w = RW_W
    row = lambda i: (i, 0)
    fix = lambda i: (0, 0)
    wide = jax.ShapeDtypeStruct((m, w), F32)
    r, lw, k, v, nkk, b, g, bonus = pl.pallas_call(
        functools.partial(_rwkv_pre_kernel, seq_len, rw_lo, ra_lo),
        out_shape=[wide] * 8,
        grid=(m // tr,),
        in_specs=[
            pl.BlockSpec((tr, RW_PA), row),
            pl.BlockSpec((SUBLANES, RW_PA), lambda i: (jnp.maximum(i * (tr // SUBLANES) - 1, 0), 0)),
            pl.BlockSpec((1, RW_PA), fix),
            pl.BlockSpec((4, w), fix),
            pl.BlockSpec((1, w), fix),
            pl.BlockSpec((LANES, w), fix),
            pl.BlockSpec((LANES, w), fix),
            pl.BlockSpec((2 * LANES, w), fix),
            pl.BlockSpec((w, LANES), fix),
            pl.BlockSpec((LANES, w), fix),
        ],
        out_specs=[pl.BlockSpec((tr, w), row)] * 8,
        compiler_params=_cparams(("parallel",)),
        name="rwkv_pre",
    )(pa, pa, mu, vec[0:4], r_k, w2, a2, g2, e_red, e_exp)

    c = RW_CHUNK
    nchunk = seq_len // c
    blk = pl.BlockSpec((c, w), lambda bi, ci: (bi * nchunk + ci, 0))
    o = pl.pallas_call(
        _rwkv_scan_kernel,
        out_shape=wide,
        grid=(batch, nchunk),
        in_specs=[blk] * 6,
        out_specs=blk,
        scratch_shapes=[pltpu.VMEM((w // LANES, LANES, LANES), F32)],
        compiler_params=_cparams(("parallel", "arbitrary")),
        name="rwkv_scan",
    )(r, lw, k, v, nkk, b)

    return pl.pallas_call(
        _rwkv_post_kernel,
        out_shape=jax.ShapeDtypeStruct((m, w), BF16),
        grid=(m // tr,),
        in_specs=[pl.BlockSpec((tr, w), row)] * 3 + [pl.BlockSpec((2, w), fix), pl.BlockSpec((w, LANES), fix),
                  pl.BlockSpec((LANES, w), fix)],
        out_specs=pl.BlockSpec((tr, w), row),
        compiler_params=_cparams(("parallel",)),
        name="rwkv_post",
    )(o, bonus, g, vec[4:6], e_red, e_exp)


def _sb_kernel(q_ref, kb_ref, vb_ref, o_ref):
    i = pl.program_id(2)
    tb = ATT_BLK
    heads = range(ATT_HG)
    hsl = [slice(hh * HD, (hh + 1) * HD) for hh in heads]
    q = [(q_ref[:, sl].astype(F32) * (HD ** -0.5 * LOG2E)).astype(BF16) for sl in hsl]
    r, c = _iota2((tb, tb), 0), _iota2((tb, tb), 1)
    upper = jnp.where(r > c, 1.0, 0.0).astype(BF16)
    diag_ok = c < r

    def block(j, carry, acc, masked):
        rows = pl.ds(pl.multiple_of(j * tb, tb), tb)
        z = [_dot_nt(q[hh], kb_ref[rows, hsl[hh]]) for hh in heads]
        nz = [-zz for zz in z]
        l1 = [jnp.log2(1.0 + jnp.exp2(jnp.minimum(z[hh], nz[hh]))) for hh in heads]
        lk = [jnp.minimum(nz[hh], 0.0) - l1[hh] for hh in heads]
        if masked:
            lk = [jnp.where(diag_ok, x, 0.0) for x in lk]
        later = [_dot(lk[hh].astype(BF16), upper) + carry[hh] for hh in heads]
        a = [jnp.exp2((jnp.minimum(z[hh], 0.0) - l1[hh]) + later[hh]) for hh in heads]
        if masked:
            a = [jnp.where(diag_ok, aa, 0.0) for aa in a]
        acc = [acc[hh] + _dot(a[hh].astype(BF16), vb_ref[rows, hsl[hh]]) for hh in heads]
        carry = [later[hh][:, 0:1] + lk[hh][:, 0:1] for hh in heads]
        return carry, acc

    st = block(i, [jnp.zeros((tb, 1), F32)] * ATT_HG, [jnp.zeros((tb, HD), F32)] * ATT_HG, True)

    def body(n, st):
        return block(i - 1 - n, st[0], st[1], False)

    carry, acc = lax.fori_loop(0, i, body, st)
    for hh in heads:
        o_ref[:, hsl[hh]] = acc[hh].astype(BF16)


def _sb_mixer(pb, batch, seq_len):
    m = pb.shape[0]
    tb = ATT_BLK
    nq = seq_len // tb
    ng = 1024 // (HD * ATT_HG)
    gw = HD * ATT_HG
    return pl.pallas_call(
        _sb_kernel,
        out_shape=jax.ShapeDtypeStruct((m, 1024), BF16),
        grid=(batch, ng, nq),
        in_specs=[
            pl.BlockSpec((tb, gw), lambda b, h, i: (b * nq + i, h)),
            pl.BlockSpec((seq_len, gw), lambda b, h, i: (b, ng + h)),
            pl.BlockSpec((seq_len, gw), lambda b, h, i: (b, 2 * ng + h)),
        ],
        out_specs=pl.BlockSpec((tb, gw), lambda b, h, i: (b * nq + i, h)),
        compiler_params=_cparams(("parallel", "parallel", "arbitrary")),
        name="stick_breaking",
    )(pb, pb, pb)


def _fox_cum_kernel(f_ref, bf_ref, cum_ref, cumt_ref):
    tb = LANES
    x = f_ref[...] + bf_ref[...]
    lf = jnp.minimum(x, 0.0) - jnp.log(1.0 + jnp.exp(-jnp.abs(x)))
    ltri = jnp.where(_tri(tb, strict=False), 1.0, 0.0).astype(BF16)
    carry = jnp.zeros((1, LANES), F32)
    for blk in range(f_ref.shape[0] // tb):
        xb = lf[blk * tb:(blk + 1) * tb, :]
        hi = xb.astype(BF16)
        mid = (xb - hi.astype(F32)).astype(BF16)
        lo = (xb - hi.astype(F32) - mid.astype(F32)).astype(BF16)
        cb = _dot(ltri, hi) + _dot(ltri, mid) + _dot(ltri, lo) + carry
        cum_ref[blk * tb:(blk + 1) * tb, :] = cb
        cumt_ref[:, blk * tb:(blk + 1) * tb] = cb.T
        carry = cb[tb - 1:tb, :]


def _fox_kernel(q_ref, k_ref, vb_ref, g_ref, cq_ref, ck_ref, o_ref, kn_ref):
    hg = pl.program_id(1)
    i = pl.program_id(2)
    tb = ATT_BLK
    heads = range(ATT_HG)
    hsl = [slice(hh * HD, (hh + 1) * HD) for hh in heads]

    def hnorm(x, g):
        x = x.astype(F32)
        return x * lax.rsqrt(jnp.mean(x * x, axis=-1, keepdims=True) + NORM_EPS) * g

    @pl.when(i == 0)
    def _():
        for sl in hsl:
            kn_ref[:, sl] = hnorm(k_ref[:, sl], g_ref[1:2, :]).astype(BF16)

    q = [(hnorm(q_ref[:, sl], g_ref[0:1, :]) * (HD ** -0.5 * LOG2E)).astype(BF16) for sl in hsl]
    cq_all = cq_ref[...] * LOG2E
    lane = _iota2((tb, LANES), 1)
    cq = [jnp.sum(jnp.where(lane == hg * ATT_HG + hh, cq_all, 0.0), axis=-1, keepdims=True) for hh in heads]
    r, c = _iota2((tb, tb), 0), _iota2((tb, tb), 1)
    causal = c <= r

    def block(j, m_run, l_run, acc, masked):
        rows = pl.ds(pl.multiple_of(j * tb, tb), tb)
        ck8 = ck_ref[:, rows]
        sub = _iota2(ck8.shape, 0)
        ck = [jnp.sum(jnp.where(sub == hg * ATT_HG + hh, ck8, 0.0), axis=0, keepdims=True) * LOG2E for hh in heads]
        s = [_dot_nt(q[hh], kn_ref[rows, hsl[hh]]) - ck[hh] for hh in heads]
        if masked:
            s = [jnp.where(causal, ss, -jnp.inf) for ss in s]
        m_new = [jnp.maximum(m_run[hh], jnp.max(s[hh], axis=-1, keepdims=True) + cq[hh]) for hh in heads]
        alpha = [jnp.exp2(m_run[hh] - m_new[hh]) for hh in heads]
        p = [jnp.exp2(s[hh] - (m_new[hh] - cq[hh])) for hh in heads]
        l_new = [alpha[hh] * l_run[hh] + jnp.sum(p[hh], axis=-1, keepdims=True) for hh in heads]
        acc = [alpha[hh] * acc[hh] + _dot(p[hh].astype(BF16), vb_ref[rows, hsl[hh]]) for hh in heads]
        return m_new, l_new, acc

    st = block(i, [jnp.full((tb, 1), -jnp.inf, F32)] * ATT_HG, [jnp.zeros((tb, 1), F32)] * ATT_HG,
               [jnp.zeros((tb, HD), F32)] * ATT_HG, True)

    def body(n, st):
        return block(i - 1 - n, st[0], st[1], st[2], False)

    m_run, l_run, acc = lax.fori_loop(0, i, body, st)
    for hh in heads:
        o_ref[:, hsl[hh]] = (acc[hh] / l_run[hh]).astype(BF16)


def _fox_mixer(pc, ps, batch, seq_len, qk_g, b_f_pad):
    m = pc.shape[0]
    tb = ATT_BLK
    nq = seq_len // tb
    cum, cumt = pl.pallas_call(
        _fox_cum_kernel,
        out_shape=[jax.ShapeDtypeStruct((m, LANES), F32), jax.ShapeDtypeStruct((batch * LANES, seq_len), F32)],
        grid=(batch,),
        in_specs=[pl.BlockSpec((seq_len, LANES), lambda b: (b, 0)), pl.BlockSpec((1, LANES), lambda b: (0, 0))],
        out_specs=[pl.BlockSpec((seq_len, LANES), lambda b: (b, 0)), pl.BlockSpec((LANES, seq_len), lambda b: (b, 0))],
        compiler_params=_cparams(("parallel",)),
        name="fox_cum",
    )(ps, b_f_pad)
    ng = 1024 // (HD * ATT_HG)
    gw = HD * ATT_HG
    return pl.pallas_call(
        _fox_kernel,
        out_shape=jax.ShapeDtypeStruct((m, 1024), BF16),
        grid=(batch, ng, nq),
        in_specs=[
            pl.BlockSpec((tb, gw), lambda b, h, i: (b * nq + i, h)),
            pl.BlockSpec((seq_len, gw), lambda b, h, i: (b, ng + h)),
            pl.BlockSpec((seq_len, gw), lambda b, h, i: (b, 2 * ng + h)),
            pl.BlockSpec((2, HD), lambda b, h, i: (0, 0)),
            pl.BlockSpec((tb, LANES), lambda b, h, i: (b * nq + i, 0)),
            pl.BlockSpec((SUBLANES, seq_len), lambda b, h, i: (b * (LANES // SUBLANES), 0)),
        ],
        out_specs=pl.BlockSpec((tb, gw), lambda b, h, i: (b * nq + i, h)),
        scratch_shapes=[pltpu.VMEM((seq_len, gw), BF16)],
        compiler_params=_cparams(("parallel", "parallel", "arbitrary")),
        name="forgetting_attention",
    )(pc, pc, pc, qk_g, cum, cumt)


HALO_BF16 = 16


def _dn_pre_kernel(seq_len, p_ref, halo_ref, s_ref, cw_ref, gp_ref, q_ref, k_ref, v_ref, gb_ref):
    tr = p_ref.shape[0]
    w = 1024
    first = (pl.program_id(0) * tr) % seq_len == 0
    x = p_ref[...].astype(F32)
    halo = jnp.where(first, 0.0, halo_ref[HALO_BF16 - SUBLANES:, :].astype(F32))
    acc = x * cw_ref[DN_CONV - 1:DN_CONV, :]
    for s in range(1, DN_CONV):
        acc = acc + _shift_rows(x, halo, s) * cw_ref[DN_CONV - 1 - s:DN_CONV - s, :]
    y = acc * _sigmoid(acc)
    for hh in range(w // HD):
        qs = y[:, hh * HD:(hh + 1) * HD]
        ks = y[:, w + hh * HD:w + (hh + 1) * HD]
        q_ref[:, hh * HD:(hh + 1) * HD] = qs * lax.rsqrt(jnp.sum(qs * qs, -1, keepdims=True) + L2_EPS) * (HD ** -0.5)
        k_ref[:, hh * HD:(hh + 1) * HD] = ks * lax.rsqrt(jnp.sum(ks * ks, -1, keepdims=True) + L2_EPS)
    v_ref[...] = y[:, 2 * w:3 * w]
    s = s_ref[...]
    nh = w // HD
    beta = _sigmoid(s)
    g = -jnp.exp(gp_ref[0:1, :]) * _softplus(s + gp_ref[1:2, :])
    gb_ref[...] = jnp.where(_iota2(s.shape, 1) < nh, beta, g)


def _dn_scan_kernel(q_ref, k_ref, v_ref, z_ref, gb_ref, ng_ref, y_ref, s_ref):
    @pl.when(pl.program_id(1) == 0)
    def _():
        s_ref[...] = jnp.zeros_like(s_ref)

    c = DN_CHUNK
    nh = 1024 // HD
    gb = gb_ref[...]
    ltri = jnp.where(_tri(c, strict=False), 1.0, 0.0).astype(BF16)
    hi = gb.astype(BF16)
    mid = (gb - hi.astype(F32)).astype(BF16)
    lo = (gb - hi.astype(F32) - mid.astype(F32)).astype(BF16)
    gc = _dot(ltri, hi) + _dot(ltri, mid) + _dot(ltri, lo)
    gct = gc.T
    strict, incl = _tri(c, strict=True), _tri(c, strict=False)
    heads = range(nh)
    sls = [slice(h * HD, (h + 1) * HD) for h in heads]
    q = [q_ref[:, sl] for sl in sls]
    k = [k_ref[:, sl] for sl in sls]
    beta = [gb[:, h:h + 1] for h in heads]
    gcol = [gc[:, nh + h:nh + h + 1] for h in heads]
    glast = [gc[c - 1:c, nh + h:nh + h + 1] for h in heads]
    eg = [jnp.exp(gcol[h]) for h in heads]
    decay = [jnp.exp(jnp.minimum(gcol[h] - gct[nh + h:nh + h + 1, :], 0.0)) for h in heads]
    kb = [k[h] * beta[h] for h in heads]
    gram = [_dot_nt(jnp.concatenate([kb[h].astype(BF16), q[h].astype(BF16)], axis=0), k[h].astype(BF16))
            for h in heads]
    t_inv = _inv_unit_lower_multi([jnp.where(strict, -gram[h][:c] * decay[h], 0.0) for h in heads], 6)
    uw = [_dot(t_inv[h].astype(BF16),
               jnp.concatenate([(v_ref[:, sls[h]] * beta[h]).astype(BF16), (kb[h] * eg[h]).astype(BF16)], axis=1))
          for h in heads]
    s0 = [s_ref[h] for h in heads]
    ws = [_dot(jnp.concatenate([uw[h][:, HD:].astype(BF16), (q[h] * eg[h]).astype(BF16)], axis=0), s0[h].astype(BF16))
          for h in heads]
    vnb = [(uw[h][:, :HD] - ws[h][:c]).astype(BF16) for h in heads]
    for h in heads:
        attn = jnp.where(incl, gram[h][c:] * decay[h], 0.0)
        o = ws[h][c:] + _dot(attn.astype(BF16), vnb[h])
        o = o * lax.rsqrt(jnp.mean(o * o, axis=-1, keepdims=True) + NORM_EPS) * ng_ref[...]
        z = z_ref[:, sls[h]].astype(F32)
        y_ref[:, sls[h]] = (o * (z * _sigmoid(z))).astype(BF16)
    for h in heads:
        s_ref[h] = s0[h] * jnp.exp(glast[h]) + _dot_tn((k[h] * jnp.exp(glast[h] - gcol[h])).astype(BF16), vnb[h])


def _dn_mixer(pd, ps, batch, seq_len, conv_w, gparams, norm_g, tr=256):
    m = pd.shape[0]
    w = 1024
    row = lambda i: (i, 0)
    fix = lambda i: (0, 0)
    wide = jax.ShapeDtypeStruct((m, w), F32)
    q, k, v, gb = pl.pallas_call(
        functools.partial(_dn_pre_kernel, seq_len),
        out_shape=[wide, wide, wide, jax.ShapeDtypeStruct((m, LANES), F32)],
        grid=(m // tr,),
        in_specs=[
            pl.BlockSpec((tr, 3 * w), row),
            pl.BlockSpec((HALO_BF16, 3 * w), lambda i: (jnp.maximum(i * (tr // HALO_BF16) - 1, 0), 0)),
            pl.BlockSpec((tr, LANES), lambda i: (i, 0)),
            pl.BlockSpec((DN_CONV, 3 * w), fix),
            pl.BlockSpec((2, LANES), fix),
        ],
        out_specs=[pl.BlockSpec((tr, w), row)] * 3 + [pl.BlockSpec((tr, LANES), row)],
        compiler_params=_cparams(("parallel",)),
        name="deltanet_pre",
    )(pd, pd, ps, conv_w, gparams)

    c = DN_CHUNK
    nchunk = seq_len // c
    blk = pl.BlockSpec((c, w), lambda bi, ci: (bi * nchunk + ci, 0))
    return pl.pallas_call(
        _dn_scan_kernel,
        out_shape=jax.ShapeDtypeStruct((m, w), BF16),
        grid=(batch, nchunk),
        in_specs=[blk, blk, blk,
                  pl.BlockSpec((c, w), lambda bi, ci: (bi * nchunk + ci, 3)),
                  pl.BlockSpec((c, LANES), lambda bi, ci: (bi * nchunk + ci, 0)),
                  pl.BlockSpec((1, HD), lambda bi, ci: (0, 0))],
        out_specs=blk,
        scratch_shapes=[pltpu.VMEM((w // HD, HD, HD), F32)],
        compiler_params=_cparams(("parallel", "arbitrary")),
        name="deltanet_scan",
    )(q, k, v, pd, gb, norm_g)


def _merge_kernel(hn_ref, ya_ref, yb_ref, yc_ref, yd_ref, wg_ref, bg_ref, wb_ref, o_ref, acc_ref):
    i = pl.program_id(1)

    @pl.when(i == 0)
    def _():
        acc_ref[...] = jnp.zeros_like(acc_ref)

    gate = _sigmoid(_dot(hn_ref[...], wg_ref[...]) + bg_ref[...])
    ys = (ya_ref, yb_ref, yc_ref, yd_ref)
    for n in range(4):
        @pl.when(i == n)
        def _(n=n):
            acc_ref[...] += gate * _dot(ys[n][...], wb_ref[...])

    @pl.when(i == 3)
    def _():
        o_ref[...] = acc_ref[...].astype(BF16)


def _outproj_kernel(x_ref, m_ref, w_ref, g_ref, o_ref):
    o_ref[...] = x_ref[...] + _rms(_dot(m_ref[...], w_ref[...]), g_ref[...])


def _merge(x, hn, g3, ys, w_gate, b_gate, w_branch, w_out, l, tm=512):
    m, d = x.shape
    bw = ys[0].shape[1]
    row = lambda r, i: (r, 0)
    merged = pl.pallas_call(
        _merge_kernel,
        out_shape=jax.ShapeDtypeStruct((m, d), BF16),
        grid=(m // tm, 4),
        in_specs=[
            pl.BlockSpec((tm, d), row),
            pl.BlockSpec((tm, bw), row),
            pl.BlockSpec((tm, bw), row),
            pl.BlockSpec((tm, bw), row),
            pl.BlockSpec((tm, bw), row),
            pl.BlockSpec((None, None, d, d), lambda r, i: (l, i, 0, 0)),
            pl.BlockSpec((None, None, 1, d), lambda r, i: (l, i, 0, 0)),
            pl.BlockSpec((None, None, bw, d), lambda r, i: (l, i, 0, 0)),
        ],
        out_specs=pl.BlockSpec((tm, d), row),
        scratch_shapes=[pltpu.VMEM((tm, d), F32)],
        compiler_params=_cparams(("parallel", "arbitrary")),
        name="merge",
    )(hn, ys[0], ys[1], ys[2], ys[3], w_gate, b_gate, w_branch)
    return pl.pallas_call(
        _outproj_kernel,
        out_shape=jax.ShapeDtypeStruct((m, d), F32),
        grid=(m // tm,),
        in_specs=[
            pl.BlockSpec((tm, d), lambda r: (r, 0)),
            pl.BlockSpec((tm, d), lambda r: (r, 0)),
            pl.BlockSpec((None, d, d), lambda r: (l, 0, 0)),
            pl.BlockSpec((1, d), lambda r: (0, 0)),
        ],
        out_specs=pl.BlockSpec((tm, d), lambda r: (r, 0)),
        compiler_params=_cparams(("parallel",)),
        name="outproj",
    )(x, merged, w_out, g3)


def _pad_cols(w, n):
    return jnp.pad(w, ((0, 0), (0, n - w.shape[1])))


def _pad_rows(w, n):
    return jnp.pad(w, ((0, n - w.shape[0]), (0, 0)))


def kernel(x, norm_g, ffn_w_in, ffn_w_out, w_in, w_gate, b_gate, w_branch, w_out, rwkv_mu, rwkv_vec, rwkv_r_k,
           rwkv_w2, rwkv_a2, rwkv_g2, fox_qk_g, fox_b_f, dn_conv, dn_A_log, dn_dt_bias, dn_norm_g):
    batch, seq_len, d = x.shape
    depth = norm_g.shape[0]
    bw = w_branch.shape[2]
    rw_lo = rwkv_w2.shape[1]
    ra_lo = rwkv_a2.shape[1]
    rg_lo = rwkv_g2.shape[1]
    nh = bw // HD
    assert bw == 1024 and rw_lo <= LANES and ra_lo <= LANES and rw_lo + ra_lo >= LANES and rg_lo == 2 * LANES
    assert seq_len % DN_CHUNK == 0 and (batch * seq_len) % 512 == 0

    rw_cols = 3 * bw + rw_lo + ra_lo + rg_lo
    sb_cols = 3 * bw
    fx_cols = 3 * bw + nh
    o_a, o_b, o_c, o_d = 0, rw_cols, rw_cols + sb_cols, rw_cols + sb_cols + fx_cols

    head_of = jnp.arange(bw) // RW_HD
    e_red = (head_of[:, None] == jnp.arange(LANES)[None, :]).astype(BF16)
    e_exp = e_red.T

    w_in_b = w_in.astype(BF16)
    ffn_w_in_b, ffn_w_out_b = ffn_w_in.astype(BF16), ffn_w_out.astype(BF16)
    w_gate_b, w_branch_b, w_out_b = w_gate.astype(BF16), w_branch.astype(BF16), w_out.astype(BF16)

    xf = x.reshape(batch * seq_len, d)
    for l in range(depth):
        g = norm_g[l]
        mu_p = jnp.pad(rwkv_mu[l], (0, RW_PA - rw_cols))[None, :]
        w2p = _pad_rows(rwkv_w2[l], LANES)
        a2p = _pad_rows(rwkv_a2[l], LANES)
        b_f_pad = jnp.pad(fox_b_f[l], (0, LANES - nh))[None, :]
        gparams = jnp.stack([jnp.pad(dn_A_log[l], (nh, LANES - 2 * nh)), jnp.pad(dn_dt_bias[l], (nh, LANES - 2 * nh))])

        xf = _ffn(xf, g[0:1], g[1:2], ffn_w_in_b, ffn_w_out_b, l, 0)

        hn = _norm(xf, g[2:3])
        pa = _proj(hn, w_in_b, l, o_a, rw_cols, RW_PA // 4, F32, name="proj_rwkv")
        pb = _proj(hn, w_in_b, l, o_b, 3 * bw, 1024, BF16, name="proj_sb")
        pc = _proj(hn, w_in_b, l, o_c, 3 * bw, 1024, BF16, name="proj_fox")
        pd = _proj(hn, w_in_b, l, o_d, 4 * bw, 1024, BF16, name="proj_dn")
        psf = _proj(hn, w_in_b, l, o_c + 3 * bw, nh, LANES, F32, name="proj_fgate")
        psd = _proj(hn, w_in_b, l, o_d + 4 * bw, 2 * nh, LANES, F32, name="proj_dgate")

        ya = _rwkv_mixer(pa, batch, seq_len, rw_lo, ra_lo, mu_p, rwkv_vec[l], rwkv_r_k[l].reshape(1, bw), w2p, a2p,
                         rwkv_g2[l], e_red, e_exp)
        yb = _sb_mixer(pb, batch, seq_len)
        yc = _fox_mixer(pc, psf, batch, seq_len, fox_qk_g[l], b_f_pad)
        yd = _dn_mixer(pd, psd, batch, seq_len, dn_conv[l], gparams, dn_norm_g[l][None, :])

        xf = _merge(xf, hn, g[3:4], (ya, yb, yc, yd), w_gate_b, b_gate[:, :, None, :], w_branch_b, w_out_b, l)

        xf = _ffn(xf, g[4:5], g[5:6], ffn_w_in_b, ffn_w_out_b, l, 1)
    return xf.reshape(batch, seq_len, d)
```

```python
import functools

import jax
import jax.numpy as jnp
from jax import lax
from jax.experimental import pallas as pl
from jax.experimental.pallas import tpu as pltpu

F32 = jnp.float32
BF16 = jnp.bfloat16

NORM_EPS = 1e-6
RW_GN_EPS = 64e-5
L2_EPS = 1e-12
LOG2E = 1.4426950408889634

LANES = 128
SUBLANES = 8
VMEM_LIMIT = 58 * 1024 * 1024

RW_HD = 64
RW_CHUNK = 64
HD = 128
DN_CHUNK = 128
DN_CONV = 4
ATT_BLK = 256
ATT_HG = 4
SCAN_CHUNKS_PER_STEP = 2


def _cparams(sem):
    return pltpu.CompilerParams(dimension_semantics=sem, vmem_limit_bytes=VMEM_LIMIT)


def _dot(a, b):
    return jnp.dot(a, b, preferred_element_type=F32)


def _dot_nt(a, b):
    return lax.dot_general(a, b, (((1,), (1,)), ((), ())), preferred_element_type=F32)


def _dot_tn(a, b):
    return lax.dot_general(a, b, (((0,), (0,)), ((), ())), preferred_element_type=F32)


def _split(x):
    hi = x.astype(BF16)
    lo = (x - hi.astype(F32)).astype(BF16)
    return hi, lo


def _dot_x2(a_exact, x):
    hi, lo = _split(x)
    return _dot(a_exact, hi) + _dot(a_exact, lo)


def _dot_2x(x, b_exact):
    hi, lo = _split(x)
    return _dot(hi, b_exact) + _dot(lo, b_exact)


def _headsum(x, e_red, e_exp):
    return _dot_2x(_dot_2x(x, e_red), e_exp)


def _dot_3(a, b):
    ah, al = _split(a)
    bh, bl = _split(b)
    return _dot(ah, bh) + _dot(ah, bl) + _dot(al, bh)


def _sigmoid(x):
    return 1.0 / (1.0 + jnp.exp(-x))


def _softplus(x):
    return jnp.maximum(x, 0.0) + jnp.log(1.0 + jnp.exp(-jnp.abs(x)))


def _rms(x, g):
    return x * lax.rsqrt(jnp.mean(x * x, axis=-1, keepdims=True) + NORM_EPS) * g


def _iota2(shape, dim):
    return lax.broadcasted_iota(jnp.int32, shape, dim)


def _tri(n, strict):
    r, c = _iota2((n, n), 0), _iota2((n, n), 1)
    return (c < r) if strict else (c <= r)


def _shift_rows(x, halo, k):
    rolled = pltpu.roll(x, k, axis=0)
    hx = pltpu.roll(halo, k, axis=0)
    top = jnp.where(_iota2(hx.shape, 0) < k, hx, rolled[0:SUBLANES])
    return jnp.concatenate([top, rolled[SUBLANES:]], axis=0)


def _inv_unit_lower_multi(n_mats, steps):
    n = n_mats[0].shape[0]
    eye = jnp.where(_iota2((n, n), 0) == _iota2((n, n), 1), 1.0, 0.0)
    ps = [eye + nm for nm in n_mats]
    ms = list(n_mats)
    for _ in range(steps):
        mbs = [m.astype(BF16) for m in ms]
        ms = [_dot(mb, mb) for mb in mbs]
        ps = [p + _dot(p.astype(BF16), m.astype(BF16)) for p, m in zip(ps, ms)]
    res = [(eye - p) + _dot_3(nm, p) for nm, p in zip(n_mats, ps)]
    return [p + _dot(p.astype(BF16), r.astype(BF16)) for p, r in zip(ps, res)]


def _inv_unit_lower(n_mat, steps):
    return _inv_unit_lower_multi([n_mat], steps)[0]


def _ffn_kernel(x_ref, g0_ref, g1_ref, wg_ref, wu_ref, wo_ref, o_ref, xn_ref):
    j = pl.program_id(1)

    @pl.when(j == 0)
    def _():
        xn_ref[...] = _rms(x_ref[...], g0_ref[...]).astype(BF16)
        o_ref[...] = jnp.zeros_like(o_ref)

    xn = xn_ref[...]
    gate = _dot(xn, wg_ref[...].astype(BF16))
    up = _dot(xn, wu_ref[...].astype(BF16))
    act = (gate * _sigmoid(gate) * up).astype(BF16)
    o_ref[...] += _dot(act, wo_ref[...].astype(BF16))

    @pl.when(j == pl.num_programs(1) - 1)
    def _():
        o_ref[...] = x_ref[...] + 0.5 * _rms(o_ref[...], g1_ref[...])


def _ffn(x, g0, g1, w_in, w_out, l, k, tm=1024, tf=256):
    m, d = x.shape
    dff = w_out.shape[2]
    nf = dff // tf
    return pl.pallas_call(
        _ffn_kernel,
        out_shape=jax.ShapeDtypeStruct((m, d), F32),
        grid=(m // tm, nf),
        in_specs=[
            pl.BlockSpec((tm, d), lambda i, j: (i, 0)),
            pl.BlockSpec((1, d), lambda i, j: (0, 0)),
            pl.BlockSpec((1, d), lambda i, j: (0, 0)),
            pl.BlockSpec((None, None, d, tf), lambda i, j: (l, k, 0, j)),
            pl.BlockSpec((None, None, d, tf), lambda i, j: (l, k, 0, j + nf)),
            pl.BlockSpec((None, None, tf, d), lambda i, j: (l, k, j, 0)),
        ],
        out_specs=pl.BlockSpec((tm, d), lambda i, j: (i, 0)),
        scratch_shapes=[pltpu.VMEM((tm, d), BF16)],
        compiler_params=_cparams(("parallel", "arbitrary")),
        name="ffn",
    )(x, g0, g1, w_in, w_in, w_out)


def _norm_kernel(x_ref, g_ref, o_ref):
    o_ref[...] = _rms(x_ref[...], g_ref[...]).astype(BF16)


def _norm(x, g, tm=512):
    m, d = x.shape
    return pl.pallas_call(
        _norm_kernel,
        out_shape=jax.ShapeDtypeStruct((m, d), BF16),
        grid=(m // tm,),
        in_specs=[pl.BlockSpec((tm, d), lambda i: (i, 0)), pl.BlockSpec((1, d), lambda i: (0, 0))],
        out_specs=pl.BlockSpec((tm, d), lambda i: (i, 0)),
        compiler_params=_cparams(("parallel",)),
        name="norm",
    )(x, g)


def _proj_kernel(shift, width, two_blocks, h_ref, *refs):
    if two_blocks:
        w0_ref, w1_ref, o_ref, ws_ref = refs
    else:
        w0_ref, o_ref, ws_ref = refs
    j, i = pl.program_id(0), pl.program_id(1)
    tn = o_ref.shape[1]

    @pl.when(i == 0)
    def _():
        w = w0_ref[...]
        if two_blocks:
            w = jnp.concatenate([w, w1_ref[...]], axis=1)
        u = pltpu.bitcast(w, jnp.uint32)
        if shift:
            u = pltpu.roll(u, u.shape[1] - shift, axis=1)
        u = u[:, :tn]
        if width % tn:
            u = jnp.where(j * tn + _iota2(u.shape, 1) < width, u, jnp.uint32(0))
        ws_ref[...] = pltpu.bitcast(u, BF16)

    o_ref[...] = _dot(h_ref[...], ws_ref[...]).astype(o_ref.dtype)


def _proj(hn, w_all, l, off, width, tn, out_dtype, tm=1024, name="proj"):
    m, d = hn.shape
    c0, shift = off // tn, off % tn
    nt = -(-width // tn)
    two_blocks = shift > 0 and shift + min(width, tn) > tn
    w_specs = [pl.BlockSpec((None, d, tn), lambda j, i: (l, 0, c0 + j))]
    if two_blocks:
        w_specs.append(pl.BlockSpec((None, d, tn), lambda j, i: (l, 0, c0 + j + 1)))
    return pl.pallas_call(
        functools.partial(_proj_kernel, shift, width, two_blocks),
        out_shape=jax.ShapeDtypeStruct((m, nt * tn), out_dtype),
        grid=(nt, m // tm),
        in_specs=[pl.BlockSpec((tm, d), lambda j, i: (i, 0))] + w_specs,
        out_specs=pl.BlockSpec((tm, tn), lambda j, i: (i, j)),
        scratch_shapes=[pltpu.VMEM((d, tn), BF16)],
        compiler_params=_cparams(("parallel", "arbitrary")),
        name=name,
    )(hn, *([w_all] * len(w_specs)))


RW_W = 1024
RW_PA = 3584


def _rwkv_pre_kernel(seq_len, rw_lo, ra_lo, p_ref, halo_ref, mu_ref, vec_ref, rk_ref, w2_ref, a2_ref, g2_ref,
                     er_ref, ex_ref, r_ref, lw_ref, k_ref, v_ref, nkk_ref, b_ref, g_ref, bonus_ref):
    tr = p_ref.shape[0]
    x = p_ref[...]
    first = (pl.program_id(0) * tr) % seq_len == 0
    halo = jnp.where(first, 0.0, halo_ref[...])
    prev = _shift_rows(x, halo, 1)
    x = x + (prev - x) * mu_ref[...]
    w = RW_W
    r, k, v = x[:, 0:w], x[:, w:2 * w], x[:, 2 * w:3 * w]
    tail = x[:, 3 * w:]
    lane = _iota2((tr, LANES), 1)
    w_lo = jnp.where(lane < rw_lo, tail[:, 0:LANES], 0.0)
    a_lo = jnp.where(lane < ra_lo, pltpu.roll(tail[:, 0:2 * LANES], 2 * LANES - rw_lo, axis=1)[:, 0:LANES], 0.0)
    g_off = rw_lo + ra_lo - LANES
    g_lo = pltpu.roll(tail[:, LANES:], 3 * LANES - g_off, axis=1)[:, 0:2 * LANES]
    w0, a0, k_k, k_a = vec_ref[0:1, :], vec_ref[1:2, :], vec_ref[2:3, :], vec_ref[3:4, :]
    lw = -jnp.exp(-0.5) * _sigmoid(w0 + _dot_3(jnp.tanh(w_lo), w2_ref[...]))
    a = _sigmoid(a0 + _dot_3(a_lo, a2_ref[...]))
    g = _dot_3(_sigmoid(g_lo), g2_ref[...])
    e_red, e_exp = er_ref[...], ex_ref[...]
    kk = k * k_k
    kk = kk * lax.rsqrt(_headsum(kk * kk, e_red, e_exp) + L2_EPS)
    k = k * (1.0 + (a - 1.0) * k_a)
    bonus = _headsum(r * k * rk_ref[...], e_red, e_exp) * v
    r_ref[...] = r
    lw_ref[...] = lw
    k_ref[...] = k
    v_ref[...] = v
    nkk_ref[...] = -kk
    b_ref[...] = kk * a
    g_ref[...] = g
    bonus_ref[...] = bonus


def _rwkv_scan_kernel(r_ref, lw_ref, k_ref, v_ref, nkk_ref, b_ref, o_ref, s_ref):
    @pl.when(pl.program_id(1) == 0)
    def _():
        s_ref[...] = jnp.zeros_like(s_ref)

    for cc in range(r_ref.shape[0] // RW_CHUNK):
        _rwkv_chunk(slice(cc * RW_CHUNK, (cc + 1) * RW_CHUNK), r_ref, lw_ref, k_ref, v_ref, nkk_ref, b_ref, o_ref, s_ref)


def _rwkv_chunk(rs, r_ref, lw_ref, k_ref, v_ref, nkk_ref, b_ref, o_ref, s_ref):
    c = RW_CHUNK
    lw = lw_ref[rs, :]
    ltri = jnp.where(_tri(c, strict=False), 1.0, 0.0).astype(BF16)
    cum = _dot_x2(ltri, lw)
    tot = cum[c - 1:c, :]
    e_in, e_neg = jnp.exp(cum), jnp.exp(-cum)
    e_ex, e_end = jnp.exp(cum - lw), jnp.exp(tot - cum)
    r, k, v, nkk, b = r_ref[rs, :], k_ref[rs, :], v_ref[rs, :], nkk_ref[rs, :], b_ref[rs, :]
    rt, kt, bt, at = r * e_in, k * e_neg, b * e_neg, nkk * e_ex
    kh, bh = k * e_end, b * e_end
    e_tot = jnp.exp(tot)

    n2 = 2 * c
    lane_lo = _iota2((c, LANES), 1) < RW_HD
    ri, ci = _iota2((n2, n2), 0) & (c - 1), _iota2((n2, n2), 1) & (c - 1)
    strict, incl = ci < ri, ci <= ri

    def stack(xp):
        return jnp.concatenate([jnp.where(lane_lo, xp, 0.0), jnp.where(lane_lo, 0.0, xp)], axis=0).astype(BF16)

    pairs = range(RW_W // LANES)
    sls = [slice(pr * LANES, (pr + 1) * LANES) for pr in pairs]
    at_s = [stack(at[:, sl]) for sl in sls]
    rt_s = [stack(rt[:, sl]) for sl in sls]
    bt_s = [stack(bt[:, sl]) for sl in sls]
    kt_s = [stack(kt[:, sl]) for sl in sls]
    v_s = [stack(v[:, sl]) for sl in sls]
    gram = [_dot_nt(jnp.concatenate([at_s[p], rt_s[p]], axis=0), jnp.concatenate([bt_s[p], kt_s[p]], axis=0))
            for p in pairs]
    t_inv = _inv_unit_lower_multi([jnp.where(strict, gm[:n2, :n2], 0.0) for gm in gram], 5)
    akv = [_dot(jnp.where(strict, gram[p][:n2, n2:], 0.0).astype(BF16), v_s[p]) for p in pairs]
    wu = [_dot(t_inv[p].astype(BF16), jnp.concatenate([at_s[p], akv[p].astype(BF16)], axis=1))
          for p in pairs]
    s0 = [s_ref[p] for p in pairs]
    ws = [_dot_nt(jnp.concatenate([wu[p][:, :LANES].astype(BF16), rt_s[p]], axis=0), s0[p].astype(BF16))
          for p in pairs]
    uv = [jnp.concatenate([(ws[p][:n2] + wu[p][:, LANES:]).astype(BF16), v_s[p]], axis=0) for p in pairs]
    for p in pairs:
        a_r = jnp.concatenate([jnp.where(incl, gram[p][n2:, :n2], 0.0), jnp.where(incl, gram[p][n2:, n2:], 0.0)], axis=1)
        o_s = ws[p][n2:] + _dot(a_r.astype(BF16), uv[p])
        o_ref[rs, sls[p]] = o_s[:c] + o_s[c:]
    for p in pairs:
        bk = jnp.concatenate([stack(bh[:, sls[p]]), stack(kh[:, sls[p]])], axis=0)
        s_ref[p] = s0[p] * e_tot[:, sls[p]] + _dot_tn(uv[p], bk)


def _rwkv_post_kernel(o_ref, bonus_ref, g_ref, ln_ref, er_ref, ex_ref, y_ref):
    o = o_ref[...]
    e_red, e_exp = er_ref[...], ex_ref[...]
    mean = _headsum(o, e_red, e_exp) * (1.0 / RW_HD)
    d = o - mean
    var = _headsum(d * d, e_red, e_exp) * (1.0 / RW_HD)
    o = d * lax.rsqrt(var + RW_GN_EPS) * ln_ref[0:1, :] + ln_ref[1:2, :]
    y_ref[...] = ((o + bonus_ref[...]) * g_ref[...]).astype(BF16)


def _rwkv_mixer(pa, batch, seq_len, rw_lo, ra_lo, mu, vec, r_k, w2, a2, g2, e_red, e_exp, tr=256):
    m = pa.shape[0]
    w = RW_W
    row = lambda i: (i, 0)
    fix = lambda i: (0, 0)
    wide = jax.ShapeDtypeStruct((m, w), F32)
    r, lw, k, v, nkk, b, g, bonus = pl.pallas_call(
        functools.partial(_rwkv_pre_kernel, seq_len, rw_lo, ra_lo),
        out_shape=[wide] * 8,
        grid=(m // tr,),
        in_specs=[
            pl.BlockSpec((tr, RW_PA), row),
            pl.BlockSpec((SUBLANES, RW_PA), lambda i: (jnp.maximum(i * (tr // SUBLANES) - 1, 0), 0)),
            pl.BlockSpec((1, RW_PA), fix),
            pl.BlockSpec((4, w), fix),
            pl.BlockSpec((1, w), fix),
            pl.BlockSpec((LANES, w), fix),
            pl.BlockSpec((LANES, w), fix),
            pl.BlockSpec((2 * LANES, w), fix),
            pl.BlockSpec((w, LANES), fix),
            pl.BlockSpec((LANES, w), fix),
        ],
        out_specs=[pl.BlockSpec((tr, w), row)] * 8,
        compiler_params=_cparams(("parallel",)),
        name="rwkv_pre",
    )(pa, pa, mu, vec[0:4], r_k, w2, a2, g2, e_red, e_exp)

    c = RW_CHUNK * SCAN_CHUNKS_PER_STEP
    nchunk = seq_len // c
    blk = pl.BlockSpec((c, w), lambda bi, ci: (bi * nchunk + ci, 0))
    o = pl.pallas_call(
        _rwkv_scan_kernel,
        out_shape=wide,
        grid=(batch, nchunk),
        in_specs=[blk] * 6,
        out_specs=blk,
        scratch_shapes=[pltpu.VMEM((w // LANES, LANES, LANES), F32)],
        compiler_params=_cparams(("parallel", "arbitrary")),
        name="rwkv_scan",
    )(r, lw, k, v, nkk, b)

    return pl.pallas_call(
        _rwkv_post_kernel,
        out_shape=jax.ShapeDtypeStruct((m, w), BF16),
        grid=(m // tr,),
        in_specs=[pl.BlockSpec((tr, w), row)] * 3 + [pl.BlockSpec((2, w), fix), pl.BlockSpec((w, LANES), fix),
                  pl.BlockSpec((LANES, w), fix)],
        out_specs=pl.BlockSpec((tr, w), row),
        compiler_params=_cparams(("parallel",)),
        name="rwkv_post",
    )(o, bonus, g, vec[4:6], e_red, e_exp)


def _sb_kernel(q_ref, kb_ref, vb_ref, o_ref):
    i = pl.program_id(2)
    tb = ATT_BLK
    heads = range(ATT_HG)
    hsl = [slice(hh * HD, (hh + 1) * HD) for hh in heads]
    q = [(q_ref[:, sl].astype(F32) * (HD ** -0.5 * LOG2E)).astype(BF16) for sl in hsl]
    r, c = _iota2((tb, tb), 0), _iota2((tb, tb), 1)
    upper = jnp.where(r > c, 1.0, 0.0).astype(BF16)
    diag_ok = c < r

    def block(j, carry, acc, masked):
        rows = pl.ds(pl.multiple_of(j * tb, tb), tb)
        z = [_dot_nt(q[hh], kb_ref[rows, hsl[hh]]) for hh in heads]
        nz = [-zz for zz in z]
        l1 = [jnp.log2(1.0 + jnp.exp2(jnp.minimum(z[hh], nz[hh]))) for hh in heads]
        lk = [jnp.minimum(nz[hh], 0.0) - l1[hh] for hh in heads]
        if masked:
            lk = [jnp.where(diag_ok, x, 0.0) for x in lk]
        later = [_dot(lk[hh].astype(BF16), upper) + carry[hh] for hh in heads]
        a = [jnp.exp2((jnp.minimum(z[hh], 0.0) - l1[hh]) + later[hh]) for hh in heads]
        if masked:
            a = [jnp.where(diag_ok, aa, 0.0) for aa in a]
        acc = [acc[hh] + _dot(a[hh].astype(BF16), vb_ref[rows, hsl[hh]]) for hh in heads]
        carry = [later[hh][:, 0:1] + lk[hh][:, 0:1] for hh in heads]
        return carry, acc

    st = block(i, [jnp.zeros((tb, 1), F32)] * ATT_HG, [jnp.zeros((tb, HD), F32)] * ATT_HG, True)

    def body(n, st):
        return block(i - 1 - n, st[0], st[1], False)

    carry, acc = lax.fori_loop(0, i, body, st)
    for hh in heads:
        o_ref[:, hsl[hh]] = acc[hh].astype(BF16)


def _sb_mixer(pb, batch, seq_len):
    m = pb.shape[0]
    tb = ATT_BLK
    nq = seq_len // tb
    ng = 1024 // (HD * ATT_HG)
    gw = HD * ATT_HG
    return pl.pallas_call(
        _sb_kernel,
        out_shape=jax.ShapeDtypeStruct((m, 1024), BF16),
        grid=(batch, ng, nq),
        in_specs=[
            pl.BlockSpec((tb, gw), lambda b, h, i: (b * nq + i, h)),
            pl.BlockSpec((seq_len, gw), lambda b, h, i: (b, ng + h)),
            pl.BlockSpec((seq_len, gw), lambda b, h, i: (b, 2 * ng + h)),
        ],
        out_specs=pl.BlockSpec((tb, gw), lambda b, h, i: (b * nq + i, h)),
        compiler_params=_cparams(("parallel", "parallel", "arbitrary")),
        name="stick_breaking",
    )(pb, pb, pb)


def _fox_cum_kernel(f_ref, bf_ref, cum_ref, cumt_ref):
    tb = LANES
    x = f_ref[...] + bf_ref[...]
    lf = jnp.minimum(x, 0.0) - jnp.log(1.0 + jnp.exp(-jnp.abs(x)))
    ltri = jnp.where(_tri(tb, strict=False), 1.0, 0.0).astype(BF16)
    carry = jnp.zeros((1, LANES), F32)
    for blk in range(f_ref.shape[0] // tb):
        xb = lf[blk * tb:(blk + 1) * tb, :]
        hi = xb.astype(BF16)
        mid = (xb - hi.astype(F32)).astype(BF16)
        lo = (xb - hi.astype(F32) - mid.astype(F32)).astype(BF16)
        cb = _dot(ltri, hi) + _dot(ltri, mid) + _dot(ltri, lo) + carry
        cum_ref[blk * tb:(blk + 1) * tb, :] = cb
        cumt_ref[:, blk * tb:(blk + 1) * tb] = cb.T
        carry = cb[tb - 1:tb, :]


def _fox_kernel(q_ref, k_ref, vb_ref, g_ref, cq_ref, ck_ref, o_ref, kn_ref):
    hg = pl.program_id(1)
    i = pl.program_id(2)
    tb = ATT_BLK
    heads = range(ATT_HG)
    hsl = [slice(hh * HD, (hh + 1) * HD) for hh in heads]

    def hnorm(x, g):
        x = x.astype(F32)
        return x * lax.rsqrt(jnp.mean(x * x, axis=-1, keepdims=True) + NORM_EPS) * g

    @pl.when(i == 0)
    def _():
        for sl in hsl:
            kn_ref[:, sl] = hnorm(k_ref[:, sl], g_ref[1:2, :]).astype(BF16)

    q = [(hnorm(q_ref[:, sl], g_ref[0:1, :]) * (HD ** -0.5 * LOG2E)).astype(BF16) for sl in hsl]
    cq_all = cq_ref[...] * LOG2E
    lane = _iota2((tb, LANES), 1)
    cq = [jnp.sum(jnp.where(lane == hg * ATT_HG + hh, cq_all, 0.0), axis=-1, keepdims=True) for hh in heads]
    r, c = _iota2((tb, tb), 0), _iota2((tb, tb), 1)
    causal = c <= r

    def block(j, m_run, l_run, acc, masked):
        rows = pl.ds(pl.multiple_of(j * tb, tb), tb)
        ck8 = ck_ref[:, rows]
        sub = _iota2(ck8.shape, 0)
        ck = [jnp.sum(jnp.where(sub == hg * ATT_HG + hh, ck8, 0.0), axis=0, keepdims=True) * LOG2E for hh in heads]
        s = [_dot_nt(q[hh], kn_ref[rows, hsl[hh]]) - ck[hh] for hh in heads]
        if masked:
            s = [jnp.where(causal, ss, -jnp.inf) for ss in s]
        m_new = [jnp.maximum(m_run[hh], jnp.max(s[hh], axis=-1, keepdims=True) + cq[hh]) for hh in heads]
        alpha = [jnp.exp2(m_run[hh] - m_new[hh]) for hh in heads]
        p = [jnp.exp2(s[hh] - (m_new[hh] - cq[hh])) for hh in heads]
        l_new = [alpha[hh] * l_run[hh] + jnp.sum(p[hh], axis=-1, keepdims=True) for hh in heads]
        acc = [alpha[hh] * acc[hh] + _dot(p[hh].astype(BF16), vb_ref[rows, hsl[hh]]) for hh in heads]
        return m_new, l_new, acc

    st = block(i, [jnp.full((tb, 1), -jnp.inf, F32)] * ATT_HG, [jnp.zeros((tb, 1), F32)] * ATT_HG,
               [jnp.zeros((tb, HD), F32)] * ATT_HG, True)

    def body(n, st):
        return block(i - 1 - n, st[0], st[1], st[2], False)

    m_run, l_run, acc = lax.fori_loop(0, i, body, st)
    for hh in heads:
        o_ref[:, hsl[hh]] = (acc[hh] / l_run[hh]).astype(BF16)


def _fox_mixer(pc, ps, batch, seq_len, qk_g, b_f_pad):
    m = pc.shape[0]
    tb = ATT_BLK
    nq = seq_len // tb
    cum, cumt = pl.pallas_call(
        _fox_cum_kernel,
        out_shape=[jax.ShapeDtypeStruct((m, LANES), F32), jax.ShapeDtypeStruct((batch * LANES, seq_len), F32)],
        grid=(batch,),
        in_specs=[pl.BlockSpec((seq_len, LANES), lambda b: (b, 0)), pl.BlockSpec((1, LANES), lambda b: (0, 0))],
        out_specs=[pl.BlockSpec((seq_len, LANES), lambda b: (b, 0)), pl.BlockSpec((LANES, seq_len), lambda b: (b, 0))],
        compiler_params=_cparams(("parallel",)),
        name="fox_cum",
    )(ps, b_f_pad)
    ng = 1024 // (HD * ATT_HG)
    gw = HD * ATT_HG
    return pl.pallas_call(
        _fox_kernel,
        out_shape=jax.ShapeDtypeStruct((m, 1024), BF16),
        grid=(batch, ng, nq),
        in_specs=[
            pl.BlockSpec((tb, gw), lambda b, h, i: (b * nq + i, h)),
            pl.BlockSpec((seq_len, gw), lambda b, h, i: (b, ng + h)),
            pl.BlockSpec((seq_len, gw), lambda b, h, i: (b, 2 * ng + h)),
            pl.BlockSpec((2, HD), lambda b, h, i: (0, 0)),
            pl.BlockSpec((tb, LANES), lambda b, h, i: (b * nq + i, 0)),
            pl.BlockSpec((SUBLANES, seq_len), lambda b, h, i: (b * (LANES // SUBLANES), 0)),
        ],
        out_specs=pl.BlockSpec((tb, gw), lambda b, h, i: (b * nq + i, h)),
        scratch_shapes=[pltpu.VMEM((seq_len, gw), BF16)],
        compiler_params=_cparams(("parallel", "parallel", "arbitrary")),
        name="forgetting_attention",
    )(pc, pc, pc, qk_g, cum, cumt)


HALO_BF16 = 16


def _dn_pre_kernel(seq_len, p_ref, halo_ref, s_ref, cw_ref, gp_ref, q_ref, k_ref, v_ref, gb_ref):
    tr = p_ref.shape[0]
    w = 1024
    first = (pl.program_id(0) * tr) % seq_len == 0
    x = p_ref[...].astype(F32)
    halo = jnp.where(first, 0.0, halo_ref[HALO_BF16 - SUBLANES:, :].astype(F32))
    acc = x * cw_ref[DN_CONV - 1:DN_CONV, :]
    for s in range(1, DN_CONV):
        acc = acc + _shift_rows(x, halo, s) * cw_ref[DN_CONV - 1 - s:DN_CONV - s, :]
    y = acc * _sigmoid(acc)
    for hh in range(w // HD):
        qs = y[:, hh * HD:(hh + 1) * HD]
        ks = y[:, w + hh * HD:w + (hh + 1) * HD]
        q_ref[:, hh * HD:(hh + 1) * HD] = qs * lax.rsqrt(jnp.sum(qs * qs, -1, keepdims=True) + L2_EPS) * (HD ** -0.5)
        k_ref[:, hh * HD:(hh + 1) * HD] = ks * lax.rsqrt(jnp.sum(ks * ks, -1, keepdims=True) + L2_EPS)
    v_ref[...] = y[:, 2 * w:3 * w]
    s = s_ref[...]
    nh = w // HD
    beta = _sigmoid(s)
    g = -jnp.exp(gp_ref[0:1, :]) * _softplus(s + gp_ref[1:2, :])
    gb_ref[...] = jnp.where(_iota2(s.shape, 1) < nh, beta, g)


def _dn_scan_kernel(q_ref, k_ref, v_ref, z_ref, gb_ref, ng_ref, y_ref, s_ref):
    @pl.when(pl.program_id(1) == 0)
    def _():
        s_ref[...] = jnp.zeros_like(s_ref)

    for cc in range(q_ref.shape[0] // DN_CHUNK):
        _dn_chunk(slice(cc * DN_CHUNK, (cc + 1) * DN_CHUNK), q_ref, k_ref, v_ref, z_ref, gb_ref, ng_ref, y_ref, s_ref)


def _dn_chunk(rs, q_ref, k_ref, v_ref, z_ref, gb_ref, ng_ref, y_ref, s_ref):
    c = DN_CHUNK
    nh = 1024 // HD
    gb = gb_ref[rs, :]
    ltri = jnp.where(_tri(c, strict=False), 1.0, 0.0).astype(BF16)
    hi = gb.astype(BF16)
    mid = (gb - hi.astype(F32)).astype(BF16)
    lo = (gb - hi.astype(F32) - mid.astype(F32)).astype(BF16)
    gc = _dot(ltri, hi) + _dot(ltri, mid) + _dot(ltri, lo)
    gct = gc.T
    strict, incl = _tri(c, strict=True), _tri(c, strict=False)
    heads = range(nh)
    sls = [slice(h * HD, (h + 1) * HD) for h in heads]
    q = [q_ref[rs, sl] for sl in sls]
    k = [k_ref[rs, sl] for sl in sls]
    beta = [gb[:, h:h + 1] for h in heads]
    gcol = [gc[:, nh + h:nh + h + 1] for h in heads]
    glast = [gc[c - 1:c, nh + h:nh + h + 1] for h in heads]
    eg = [jnp.exp(gcol[h]) for h in heads]
    decay = [jnp.exp(jnp.minimum(gcol[h] - gct[nh + h:nh + h + 1, :], 0.0)) for h in heads]
    kb = [k[h] * beta[h] for h in heads]
    gram = [_dot_nt(jnp.concatenate([kb[h].astype(BF16), q[h].astype(BF16)], axis=0), k[h].astype(BF16))
            for h in heads]
    t_inv = _inv_unit_lower_multi([jnp.where(strict, -gram[h][:c] * decay[h], 0.0) for h in heads], 6)
    uw = [_dot(t_inv[h].astype(BF16),
               jnp.concatenate([(v_ref[rs, sls[h]] * beta[h]).astype(BF16), (kb[h] * eg[h]).astype(BF16)], axis=1))
          for h in heads]
    s0 = [s_ref[h] for h in heads]
    ws = [_dot(jnp.concatenate([uw[h][:, HD:].astype(BF16), (q[h] * eg[h]).astype(BF16)], axis=0), s0[h].astype(BF16))
          for h in heads]
    vnb = [(uw[h][:, :HD] - ws[h][:c]).astype(BF16) for h in heads]
    for h in heads:
        attn = jnp.where(incl, gram[h][c:] * decay[h], 0.0)
        o = ws[h][c:] + _dot(attn.astype(BF16), vnb[h])
        o = o * lax.rsqrt(jnp.mean(o * o, axis=-1, keepdims=True) + NORM_EPS) * ng_ref[...]
        z = z_ref[rs, sls[h]].astype(F32)
        y_ref[rs, sls[h]] =(o * (z * _sigmoid(z))).astype(BF16)
    for h in heads:
        s_ref[h] = s0[h] * jnp.exp(glast[h]) + _dot_tn((k[h] * jnp.exp(glast[h] - gcol[h])).astype(BF16), vnb[h])


def _dn_mixer(pd, ps, batch, seq_len, conv_w, gparams, norm_g, tr=256):
    m = pd.shape[0]
    w = 1024
    row = lambda i: (i, 0)
    fix = lambda i: (0, 0)
    wide = jax.ShapeDtypeStruct((m, w), F32)
    q, k, v, gb = pl.pallas_call(
        functools.partial(_dn_pre_kernel, seq_len),
        out_shape=[wide, wide, wide, jax.ShapeDtypeStruct((m, LANES), F32)],
        grid=(m // tr,),
        in_specs=[
            pl.BlockSpec((tr, 3 * w), row),
            pl.BlockSpec((HALO_BF16, 3 * w), lambda i: (jnp.maximum(i * (tr // HALO_BF16) - 1, 0), 0)),
            pl.BlockSpec((tr, LANES), lambda i: (i, 0)),
            pl.BlockSpec((DN_CONV, 3 * w), fix),
            pl.BlockSpec((2, LANES), fix),
        ],
        out_specs=[pl.BlockSpec((tr, w), row)] * 3 + [pl.BlockSpec((tr, LANES), row)],
        compiler_params=_cparams(("parallel",)),
        name="deltanet_pre",
    )(pd, pd, ps, conv_w, gparams)

    c = DN_CHUNK * SCAN_CHUNKS_PER_STEP
    nchunk = seq_len // c
    blk = pl.BlockSpec((c, w), lambda bi, ci: (bi * nchunk + ci, 0))
    return pl.pallas_call(
        _dn_scan_kernel,
        out_shape=jax.ShapeDtypeStruct((m, w), BF16),
        grid=(batch, nchunk),
        in_specs=[blk, blk, blk,
                  pl.BlockSpec((c, w), lambda bi, ci: (bi * nchunk + ci, 3)),
                  pl.BlockSpec((c, LANES), lambda bi, ci: (bi * nchunk + ci, 0)),
                  pl.BlockSpec((1, HD), lambda bi, ci: (0, 0))],
        out_specs=blk,
        scratch_shapes=[pltpu.VMEM((w // HD, HD, HD), F32)],
        compiler_params=_cparams(("parallel", "arbitrary")),
        name="deltanet_scan",
    )(q, k, v, pd, gb, norm_g)


def _merge_kernel(hn_ref, ya_ref, yb_ref, yc_ref, yd_ref, wg_ref, bg_ref, wb_ref, o_ref, acc_ref):
    i = pl.program_id(1)

    @pl.when(i == 0)
    def _():
        acc_ref[...] = jnp.zeros_like(acc_ref)

    gate = _sigmoid(_dot(hn_ref[...], wg_ref[...]) + bg_ref[...])
    ys = (ya_ref, yb_ref, yc_ref, yd_ref)
    for n in range(4):
        @pl.when(i == n)
        def _(n=n):
            acc_ref[...] += gate * _dot(ys[n][...], wb_ref[...])

    @pl.when(i == 3)
    def _():
        o_ref[...] = acc_ref[...].astype(BF16)


def _outproj_kernel(x_ref, m_ref, w_ref, g_ref, o_ref):
    o_ref[...] = x_ref[...] + _rms(_dot(m_ref[...], w_ref[...]), g_ref[...])


def _merge(x, hn, g3, ys, w_gate, b_gate, w_branch, w_out, l, tm=512):
    m, d = x.shape
    bw = ys[0].shape[1]
    row = lambda r, i: (r, 0)
    merged = pl.pallas_call(
        _merge_kernel,
        out_shape=jax.ShapeDtypeStruct((m, d), BF16),
        grid=(m // tm, 4),
        in_specs=[
            pl.BlockSpec((tm, d), row),
            pl.BlockSpec((tm, bw), row),
            pl.BlockSpec((tm, bw), row),
            pl.BlockSpec((tm, bw), row),
            pl.BlockSpec((tm, bw), row),
            pl.BlockSpec((None, None, d, d), lambda r, i: (l, i, 0, 0)),
            pl.BlockSpec((None, None, 1, d), lambda r, i: (l, i, 0, 0)),
            pl.BlockSpec((None, None, bw, d), lambda r, i: (l, i, 0, 0)),
        ],
        out_specs=pl.BlockSpec((tm, d), row),
        scratch_shapes=[pltpu.VMEM((tm, d), F32)],
        compiler_params=_cparams(("parallel", "arbitrary")),
        name="merge",
    )(hn, ys[0], ys[1], ys[2], ys[3], w_gate, b_gate, w_branch)
    return pl.pallas_call(
        _outproj_kernel,
        out_shape=jax.ShapeDtypeStruct((m, d), F32),
        grid=(m // tm,),
        in_specs=[
            pl.BlockSpec((tm, d), lambda r: (r, 0)),
            pl.BlockSpec((tm, d), lambda r: (r, 0)),
            pl.BlockSpec((None, d, d), lambda r: (l, 0, 0)),
            pl.BlockSpec((1, d), lambda r: (0, 0)),
        ],
        out_specs=pl.BlockSpec((tm, d), lambda r: (r, 0)),
        compiler_params=_cparams(("parallel",)),
        name="outproj",
    )(x, merged, w_out, g3)


def _pad_cols(w, n):
    return jnp.pad(w, ((0, 0), (0, n - w.shape[1])))


def _pad_rows(w, n):
    return jnp.pad(w, ((0, n - w.shape[0]), (0, 0)))


def kernel(x, norm_g, ffn_w_in, ffn_w_out, w_in, w_gate, b_gate, w_branch, w_out, rwkv_mu, rwkv_vec, rwkv_r_k,
           rwkv_w2, rwkv_a2, rwkv_g2, fox_qk_g, fox_b_f, dn_conv, dn_A_log, dn_dt_bias, dn_norm_g):
    batch, seq_len, d = x.shape
    depth = norm_g.shape[0]
    bw = w_branch.shape[2]
    rw_lo = rwkv_w2.shape[1]
    ra_lo = rwkv_a2.shape[1]
    rg_lo = rwkv_g2.shape[1]
    nh = bw // HD
    assert bw == 1024 and rw_lo <= LANES and ra_lo <= LANES and rw_lo + ra_lo >= LANES and rg_lo == 2 * LANES
    assert seq_len % (DN_CHUNK * SCAN_CHUNKS_PER_STEP) == 0 and seq_len % ATT_BLK == 0 and (batch * seq_len) % 1024 == 0

    rw_cols = 3 * bw + rw_lo + ra_lo + rg_lo
    sb_cols = 3 * bw
    fx_cols = 3 * bw + nh
    o_a, o_b, o_c, o_d = 0, rw_cols, rw_cols + sb_cols, rw_cols + sb_cols + fx_cols

    head_of = jnp.arange(bw) // RW_HD
    e_red = (head_of[:, None] == jnp.arange(LANES)[None, :]).astype(BF16)
    e_exp = e_red.T

    w_in_b = w_in.astype(BF16)
    w_gate_b, w_branch_b, w_out_b = w_gate.astype(BF16), w_branch.astype(BF16), w_out.astype(BF16)

    xf = x.reshape(batch * seq_len, d)
    for l in range(depth):
        g = norm_g[l]
        mu_p = jnp.pad(rwkv_mu[l], (0, RW_PA - rw_cols))[None, :]
        w2p = _pad_rows(rwkv_w2[l], LANES)
        a2p = _pad_rows(rwkv_a2[l], LANES)
        b_f_pad = jnp.pad(fox_b_f[l], (0, LANES - nh))[None, :]
        gparams = jnp.stack([jnp.pad(dn_A_log[l], (nh, LANES - 2 * nh)), jnp.pad(dn_dt_bias[l], (nh, LANES - 2 * nh))])

        xf = _ffn(xf, g[0:1], g[1:2], ffn_w_in, ffn_w_out, l, 0)

        hn = _norm(xf, g[2:3])
        pa = _proj(hn, w_in_b, l, o_a, rw_cols, RW_PA // 4, F32, name="proj_rwkv")
        pb = _proj(hn, w_in_b, l, o_b, 3 * bw, 1024, BF16, name="proj_sb")
        pc = _proj(hn, w_in_b, l, o_c, 3 * bw, 1024, BF16, name="proj_fox")
        pd = _proj(hn, w_in_b, l, o_d, 4 * bw, 1024, BF16, name="proj_dn")
        psf = _proj(hn, w_in_b, l, o_c + 3 * bw, nh, LANES, F32, name="proj_fgate")
        psd = _proj(hn, w_in_b, l, o_d + 4 * bw, 2 * nh, LANES, F32, name="proj_dgate")

        ya = _rwkv_mixer(pa, batch, seq_len, rw_lo, ra_lo, mu_p, rwkv_vec[l], rwkv_r_k[l].reshape(1, bw), w2p, a2p,
                         rwkv_g2[l], e_red, e_exp)
        yb = _sb_mixer(pb, batch, seq_len)
        yc = _fox_mixer(pc, psf, batch, seq_len, fox_qk_g[l], b_f_pad)
        yd = _dn_mixer(pd, psd, batch, seq_len, dn_conv[l], gparams, dn_norm_g[l][None, :])

        xf = _merge(xf, hn, g[3:4], (ya, yb, yc, yd), w_gate_b, b_gate[:, :, None, :], w_branch_b, w_out_b, l)

        xf = _ffn(xf, g[4:5], g[5:6], ffn_w_in, ffn_w_out, l, 1)
    return xf.reshape(batch, seq_len, d)
```

```python
import functools

import jax
import jax.numpy as jnp
from jax import lax
from jax.experimental import pallas as pl
from jax.experimental.pallas import tpu as pltpu

F32 = jnp.float32
BF16 = jnp.bfloat16

NORM_EPS = 1e-6
RW_GN_EPS = 64e-5
L2_EPS = 1e-12
LOG2E = 1.4426950408889634

LANES = 128
SUBLANES = 8
VMEM_LIMIT = 58 * 1024 * 1024

RW_HD = 64
RW_CHUNK = 64
HD = 128
DN_CHUNK = 128
DN_CONV = 4
ATT_BLK = 256
ATT_HG = 4
SCAN_CHUNKS_PER_STEP = 2


def _cparams(sem):
    return pltpu.CompilerParams(dimension_semantics=sem, vmem_limit_bytes=VMEM_LIMIT)


def _dot(a, b):
    return jnp.dot(a, b, preferred_element_type=F32)


def _dot_nt(a, b):
    return lax.dot_general(a, b, (((1,), (1,)), ((), ())), preferred_element_type=F32)


def _dot_tn(a, b):
    return lax.dot_general(a, b, (((0,), (0,)), ((), ())), preferred_element_type=F32)


def _split(x):
    hi = x.astype(BF16)
    lo = (x - hi.astype(F32)).astype(BF16)
    return hi, lo


def _dot_x2(a_exact, x):
    hi, lo = _split(x)
    return _dot(a_exact, hi) + _dot(a_exact, lo)


def _dot_2x(x, b_exact):
    hi, lo = _split(x)
    return _dot(hi, b_exact) + _dot(lo, b_exact)


def _headsum(x, e_red, e_exp):
    return _dot_2x(_dot_2x(x, e_red), e_exp)


def _dot_3(a, b):
    ah, al = _split(a)
    bh, bl = _split(b)
    return _dot(ah, bh) + _dot(ah, bl) + _dot(al, bh)


def _sigmoid(x):
    return 1.0 / (1.0 + jnp.exp(-x))


def _softplus(x):
    return jnp.maximum(x, 0.0) + jnp.log(1.0 + jnp.exp(-jnp.abs(x)))


def _rms(x, g):
    return x * lax.rsqrt(jnp.mean(x * x, axis=-1, keepdims=True) + NORM_EPS) * g


def _iota2(shape, dim):
    return lax.broadcasted_iota(jnp.int32, shape, dim)


def _tri(n, strict):
    r, c = _iota2((n, n), 0), _iota2((n, n), 1)
    return (c < r) if strict else (c <= r)


def _shift_rows(x, halo, k):
    rolled = pltpu.roll(x, k, axis=0)
    hx = pltpu.roll(halo, k, axis=0)
    top = jnp.where(_iota2(hx.shape, 0) < k, hx, rolled[0:SUBLANES])
    return jnp.concatenate([top, rolled[SUBLANES:]], axis=0)


def _inv_unit_lower_multi(n_mats, steps, refine):
    n = n_mats[0].shape[0]
    eye = jnp.where(_iota2((n, n), 0) == _iota2((n, n), 1), 1.0, 0.0)
    ps = [eye + nm for nm in n_mats]
    ms = list(n_mats)
    for _ in range(steps):
        mbs = [m.astype(BF16) for m in ms]
        ms = [_dot(mb, mb) for mb in mbs]
        ps = [p + _dot(p.astype(BF16), m.astype(BF16)) for p, m in zip(ps, ms)]
    if not refine:
        return ps
    res = [(eye - p) + _dot_3(nm, p) for nm, p in zip(n_mats, ps)]
    return [p + _dot(p.astype(BF16), r.astype(BF16)) for p, r in zip(ps, res)]


def _ffn_kernel(x_ref, g0_ref, g1_ref, wg_ref, wu_ref, wo_ref, o_ref, xn_ref):
    j = pl.program_id(1)

    @pl.when(j == 0)
    def _():
        xn_ref[...] = _rms(x_ref[...], g0_ref[...]).astype(BF16)
        o_ref[...] = jnp.zeros_like(o_ref)

    xn = xn_ref[...]
    gate = _dot(xn, wg_ref[...].astype(BF16))
    up = _dot(xn, wu_ref[...].astype(BF16))
    act = (gate * _sigmoid(gate) * up).astype(BF16)
    o_ref[...] += _dot(act, wo_ref[...].astype(BF16))

    @pl.when(j == pl.num_programs(1) - 1)
    def _():
        o_ref[...] = x_ref[...] + 0.5 * _rms(o_ref[...], g1_ref[...])


def _ffn(x, g0, g1, w_in, w_out, l, k, tm=1024, tf=256):
    m, d = x.shape
    dff = w_out.shape[2]
    nf = dff // tf
    return pl.pallas_call(
        _ffn_kernel,
        out_shape=jax.ShapeDtypeStruct((m, d), F32),
        grid=(m // tm, nf),
        in_specs=[
            pl.BlockSpec((tm, d), lambda i, j: (i, 0)),
            pl.BlockSpec((1, d), lambda i, j: (0, 0)),
            pl.BlockSpec((1, d), lambda i, j: (0, 0)),
            pl.BlockSpec((None, None, d, tf), lambda i, j: (l, k, 0, j)),
            pl.BlockSpec((None, None, d, tf), lambda i, j: (l, k, 0, j + nf)),
            pl.BlockSpec((None, None, tf, d), lambda i, j: (l, k, j, 0)),
        ],
        out_specs=pl.BlockSpec((tm, d), lambda i, j: (i, 0)),
        scratch_shapes=[pltpu.VMEM((tm, d), BF16)],
        compiler_params=_cparams(("parallel", "arbitrary")),
        name="ffn",
    )(x, g0, g1, w_in, w_in, w_out)


def _norm_kernel(x_ref, g_ref, o_ref):
    o_ref[...] = _rms(x_ref[...], g_ref[...]).astype(BF16)


def _norm(x, g, tm=512):
    m, d = x.shape
    return pl.pallas_call(
        _norm_kernel,
        out_shape=jax.ShapeDtypeStruct((m, d), BF16),
        grid=(m // tm,),
        in_specs=[pl.BlockSpec((tm, d), lambda i: (i, 0)), pl.BlockSpec((1, d), lambda i: (0, 0))],
        out_specs=pl.BlockSpec((tm, d), lambda i: (i, 0)),
        compiler_params=_cparams(("parallel",)),
        name="norm",
    )(x, g)


def _proj_kernel(shift, width, two_blocks, h_ref, *refs):
    if two_blocks:
        w0_ref, w1_ref, o_ref, ws_ref = refs
    else:
        w0_ref, o_ref, ws_ref = refs
    j, i = pl.program_id(0), pl.program_id(1)
    tn = o_ref.shape[1]

    @pl.when(i == 0)
    def _():
        w = w0_ref[...]
        if two_blocks:
            w = jnp.concatenate([w, w1_ref[...]], axis=1)
        u = pltpu.bitcast(w, jnp.uint32)
        if shift:
            u = pltpu.roll(u, u.shape[1] - shift, axis=1)
        u = u[:, :tn]
        if width % tn:
            u = jnp.where(j * tn + _iota2(u.shape, 1) < width, u, jnp.uint32(0))
        ws_ref[...] = pltpu.bitcast(u, BF16)

    o_ref[...] = _dot(h_ref[...], ws_ref[...]).astype(o_ref.dtype)


def _proj(hn, w_all, l, off, width, tn, out_dtype, tm=1024, name="proj"):
    m, d = hn.shape
    c0, shift = off // tn, off % tn
    nt = -(-width // tn)
    two_blocks = shift > 0 and shift + min(width, tn) > tn
    w_specs = [pl.BlockSpec((None, d, tn), lambda j, i: (l, 0, c0 + j))]
    if two_blocks:
        w_specs.append(pl.BlockSpec((None, d, tn), lambda j, i: (l, 0, c0 + j + 1)))
    return pl.pallas_call(
        functools.partial(_proj_kernel, shift, width, two_blocks),
        out_shape=jax.ShapeDtypeStruct((m, nt * tn), out_dtype),
        grid=(nt, m // tm),
        in_specs=[pl.BlockSpec((tm, d), lambda j, i: (i, 0))] + w_specs,
        out_specs=pl.BlockSpec((tm, tn), lambda j, i: (i, j)),
        scratch_shapes=[pltpu.VMEM((d, tn), BF16)],
        compiler_params=_cparams(("parallel", "arbitrary")),
        name=name,
    )(hn, *([w_all] * len(w_specs)))


RW_W = 1024
RW_PA = 3584


def _rwkv_pre_kernel(seq_len, rw_lo, ra_lo, p_ref, halo_ref, mu_ref, vec_ref, rk_ref, w2_ref, a2_ref, g2_ref,
                     er_ref, ex_ref, r_ref, lw_ref, k_ref, v_ref, nkk_ref, b_ref, g_ref, bonus_ref):
    tr = p_ref.shape[0]
    x = p_ref[...]
    first = (pl.program_id(0) * tr) % seq_len == 0
    halo = jnp.where(first, 0.0, halo_ref[...])
    prev = _shift_rows(x, halo, 1)
    x = x + (prev - x) * mu_ref[...]
    w = RW_W
    r, k, v = x[:, 0:w], x[:, w:2 * w], x[:, 2 * w:3 * w]
    tail = x[:, 3 * w:]
    lane = _iota2((tr, LANES), 1)
    w_lo = jnp.where(lane < rw_lo, tail[:, 0:LANES], 0.0)
    a_lo = jnp.where(lane < ra_lo, pltpu.roll(tail[:, 0:2 * LANES], 2 * LANES - rw_lo, axis=1)[:, 0:LANES], 0.0)
    g_off = rw_lo + ra_lo - LANES
    g_lo = pltpu.roll(tail[:, LANES:], 3 * LANES - g_off, axis=1)[:, 0:2 * LANES]
    w0, a0, k_k, k_a = vec_ref[0:1, :], vec_ref[1:2, :], vec_ref[2:3, :], vec_ref[3:4, :]
    lw = -jnp.exp(-0.5) * _sigmoid(w0 + _dot_3(jnp.tanh(w_lo), w2_ref[...]))
    a = _sigmoid(a0 + _dot_3(a_lo, a2_ref[...]))
    g = _dot_3(_sigmoid(g_lo), g2_ref[...])
    e_red, e_exp = er_ref[...], ex_ref[...]
    kk = k * k_k
    kk = kk * lax.rsqrt(_headsum(kk * kk, e_red, e_exp) + L2_EPS)
    k = k * (1.0 + (a - 1.0) * k_a)
    bonus = _headsum(r * k * rk_ref[...], e_red, e_exp) * v
    r_ref[...] = r
    lw_ref[...] = lw
    k_ref[...] = k
    v_ref[...] = v
    nkk_ref[...] = -kk
    b_ref[...] = kk * a
    g_ref[...] = g
    bonus_ref[...] = bonus


def _rwkv_scan_kernel(r_ref, lw_ref, k_ref, v_ref, nkk_ref, b_ref, o_ref, s_ref):
    @pl.when(pl.program_id(1) == 0)
    def _():
        s_ref[...] = jnp.zeros_like(s_ref)

    for cc in range(r_ref.shape[0] // RW_CHUNK):
        _rwkv_chunk(slice(cc * RW_CHUNK, (cc + 1) * RW_CHUNK), r_ref, lw_ref, k_ref, v_ref, nkk_ref, b_ref, o_ref, s_ref)


def _rwkv_chunk(rs, r_ref, lw_ref, k_ref, v_ref, nkk_ref, b_ref, o_ref, s_ref):
    c = RW_CHUNK
    lw = lw_ref[rs, :]
    ltri = jnp.where(_tri(c, strict=False), 1.0, 0.0).astype(BF16)
    cum = _dot_x2(ltri, lw)
    tot = cum[c - 1:c, :]
    e_in, e_neg = jnp.exp(cum), jnp.exp(-cum)
    e_ex, e_end = jnp.exp(cum - lw), jnp.exp(tot - cum)
    r, k, v, nkk, b = r_ref[rs, :], k_ref[rs, :], v_ref[rs, :], nkk_ref[rs, :], b_ref[rs, :]
    rt, kt, bt, at = r * e_in, k * e_neg, b * e_neg, nkk * e_ex
    kh, bh = k * e_end, b * e_end
    e_tot = jnp.exp(tot)

    n2 = 2 * c
    lane_lo = _iota2((c, LANES), 1) < RW_HD
    ri, ci = _iota2((n2, n2), 0) & (c - 1), _iota2((n2, n2), 1) & (c - 1)
    strict, incl = ci < ri, ci <= ri

    def stack(xp):
        return jnp.concatenate([jnp.where(lane_lo, xp, 0.0), jnp.where(lane_lo, 0.0, xp)], axis=0).astype(BF16)

    pairs = range(RW_W // LANES)
    sls = [slice(pr * LANES, (pr + 1) * LANES) for pr in pairs]
    at_s = [stack(at[:, sl]) for sl in sls]
    rt_s = [stack(rt[:, sl]) for sl in sls]
    bt_s = [stack(bt[:, sl]) for sl in sls]
    kt_s = [stack(kt[:, sl]) for sl in sls]
    v_s = [stack(v[:, sl]) for sl in sls]
    gram = [_dot_nt(jnp.concatenate([at_s[p], rt_s[p]], axis=0), jnp.concatenate([bt_s[p], kt_s[p]], axis=0))
            for p in pairs]
    t_inv = _inv_unit_lower_multi([jnp.where(strict, gm[:n2, :n2], 0.0) for gm in gram], 5, refine=False)
    akv = [_dot(jnp.where(strict, gram[p][:n2, n2:], 0.0).astype(BF16), v_s[p]) for p in pairs]
    s0 = [s_ref[p] for p in pairs]
    ws = [_dot_nt(jnp.concatenate([at_s[p], rt_s[p]], axis=0), s0[p].astype(BF16)) for p in pairs]
    u = [_dot(t_inv[p].astype(BF16), (ws[p][:n2] + akv[p]).astype(BF16)) for p in pairs]
    uv = [jnp.concatenate([u[p].astype(BF16), v_s[p]], axis=0) for p in pairs]
    for p in pairs:
        a_r = jnp.concatenate([jnp.where(incl, gram[p][n2:, :n2], 0.0), jnp.where(incl, gram[p][n2:, n2:], 0.0)], axis=1)
        o_s = ws[p][n2:] + _dot(a_r.astype(BF16), uv[p])
        o_ref[rs, sls[p]] = o_s[:c] + o_s[c:]
    for p in pairs:
        bk = jnp.concatenate([stack(bh[:, sls[p]]), stack(kh[:, sls[p]])], axis=0)
        s_ref[p] = s0[p] * e_tot[:, sls[p]] + _dot_tn(uv[p], bk)


def _rwkv_post_kernel(o_ref, bonus_ref, g_ref, ln_ref, er_ref, ex_ref, y_ref):
    o = o_ref[...]
    e_red, e_exp = er_ref[...], ex_ref[...]
    mean = _headsum(o, e_red, e_exp) * (1.0 / RW_HD)
    d = o - mean
    var = _headsum(d * d, e_red, e_exp) * (1.0 / RW_HD)
    o = d * lax.rsqrt(var + RW_GN_EPS) * ln_ref[0:1, :] + ln_ref[1:2, :]
    y_ref[...] = ((o + bonus_ref[...]) * g_ref[...]).astype(BF16)


def _rwkv_mixer(pa, batch, seq_len, rw_lo, ra_lo, mu, vec, r_k, w2, a2, g2, e_red, e_exp, tr=256):
    m = pa.shape[0]
    w = RW_W
    row = lambda i: (i, 0)
    fix = lambda i: (0, 0)
    wide = jax.ShapeDtypeStruct((m, w), F32)
    r, lw, k, v, nkk, b, g, bonus = pl.pallas_call(
        functools.partial(_rwkv_pre_kernel, seq_len, rw_lo, ra_lo),
        out_shape=[wide] * 8,
        grid=(m // tr,),
        in_specs=[
            pl.BlockSpec((tr, RW_PA), row),
            pl.BlockSpec((SUBLANES, RW_PA), lambda i: (jnp.maximum(i * (tr // SUBLANES) - 1, 0), 0)),
            pl.BlockSpec((1, RW_PA), fix),
            pl.BlockSpec((4, w), fix),
            pl.BlockSpec((1, w), fix),
            pl.BlockSpec((LANES, w), fix),
            pl.BlockSpec((LANES, w), fix),
            pl.BlockSpec((2 * LANES, w), fix),
            pl.BlockSpec((w, LANES), fix),
            pl.BlockSpec((LANES, w), fix),
        ],
        out_specs=[pl.BlockSpec((tr, w), row)] * 8,
        compiler_params=_cparams(("parallel",)),
        name="rwkv_pre",
    )(pa, pa, mu, vec[0:4], r_k, w2, a2, g2, e_red, e_exp)

    c = RW_CHUNK * SCAN_CHUNKS_PER_STEP
    nchunk = seq_len // c
    blk = pl.BlockSpec((c, w), lambda bi, ci: (bi * nchunk + ci, 0))
    o = pl.pallas_call(
        _rwkv_scan_kernel,
        out_shape=wide,
        grid=(batch, nchunk),
        in_specs=[blk] * 6,
        out_specs=blk,
        scratch_shapes=[pltpu.VMEM((w // LANES, LANES, LANES), F32)],
        compiler_params=_cparams(("parallel", "arbitrary")),
        name="rwkv_scan",
    )(r, lw, k, v, nkk, b)

    return pl.pallas_call(
        _rwkv_post_kernel,
        out_shape=jax.ShapeDtypeStruct((m, w), BF16),
        grid=(m // tr,),
        in_specs=[pl.BlockSpec((tr, w), row)] * 3 + [pl.BlockSpec((2, w), fix), pl.BlockSpec((w, LANES), fix),
                  pl.BlockSpec((LANES, w), fix)],
        out_specs=pl.BlockSpec((tr, w), row),
        compiler_params=_cparams(("parallel",)),
        name="rwkv_post",
    )(o, bonus, g, vec[4:6], e_red, e_exp)


def _sb_kernel(q_ref, kb_ref, vb_ref, o_ref):
    i = pl.program_id(2)
    tb = ATT_BLK
    heads = range(ATT_HG)
    hsl = [slice(hh * HD, (hh + 1) * HD) for hh in heads]
    q = [(q_ref[:, sl].astype(F32) * (HD ** -0.5 * LOG2E)).astype(BF16) for sl in hsl]
    r, c = _iota2((tb, tb), 0), _iota2((tb, tb), 1)
    upper = jnp.where(r > c, 1.0, 0.0).astype(BF16)
    diag_ok = c < r

    def block(j, carry, acc, masked):
        rows = pl.ds(pl.multiple_of(j * tb, tb), tb)
        z = [_dot_nt(q[hh], kb_ref[rows, hsl[hh]]) for hh in heads]
        nz = [-zz for zz in z]
        l1 = [jnp.log2(1.0 + jnp.exp2(jnp.minimum(z[hh], nz[hh]))) for hh in heads]
        lk = [jnp.minimum(nz[hh], 0.0) - l1[hh] for hh in heads]
        if masked:
            lk = [jnp.where(diag_ok, x, 0.0) for x in lk]
        later = [_dot(lk[hh].astype(BF16), upper) + carry[hh] for hh in heads]
        a = [jnp.exp2((jnp.minimum(z[hh], 0.0) - l1[hh]) + later[hh]) for hh in heads]
        if masked:
            a = [jnp.where(diag_ok, aa, 0.0) for aa in a]
        acc = [acc[hh] + _dot(a[hh].astype(BF16), vb_ref[rows, hsl[hh]]) for hh in heads]
        carry = [later[hh][:, 0:1] + lk[hh][:, 0:1] for hh in heads]
        return carry, acc

    st = block(i, [jnp.zeros((tb, 1), F32)] * ATT_HG, [jnp.zeros((tb, HD), F32)] * ATT_HG, True)

    def body(n, st):
        return block(i - 1 - n, st[0], st[1], False)

    carry, acc = lax.fori_loop(0, i, body, st)
    for hh in heads:
        o_ref[:, hsl[hh]] = acc[hh].astype(BF16)


def _sb_mixer(pb, batch, seq_len):
    m = pb.shape[0]
    tb = ATT_BLK
    nq = seq_len // tb
    ng = 1024 // (HD * ATT_HG)
    gw = HD * ATT_HG
    return pl.pallas_call(
        _sb_kernel,
        out_shape=jax.ShapeDtypeStruct((m, 1024), BF16),
        grid=(batch, ng, nq),
        in_specs=[
            pl.BlockSpec((tb, gw), lambda b, h, i: (b * nq + i, h)),
            pl.BlockSpec((seq_len, gw), lambda b, h, i: (b, ng + h)),
            pl.BlockSpec((seq_len, gw), lambda b, h, i: (b, 2 * ng + h)),
        ],
        out_specs=pl.BlockSpec((tb, gw), lambda b, h, i: (b * nq + i, h)),
        compiler_params=_cparams(("parallel", "parallel", "arbitrary")),
        name="stick_breaking",
    )(pb, pb, pb)


def _fox_cum_kernel(f_ref, bf_ref, cum_ref, cumt_ref):
    tb = LANES
    x = f_ref[...] + bf_ref[...]
    lf = jnp.minimum(x, 0.0) - jnp.log(1.0 + jnp.exp(-jnp.abs(x)))
    ltri = jnp.where(_tri(tb, strict=False), 1.0, 0.0).astype(BF16)
    carry = jnp.zeros((1, LANES), F32)
    for blk in range(f_ref.shape[0] // tb):
        xb = lf[blk * tb:(blk + 1) * tb, :]
        hi = xb.astype(BF16)
        mid = (xb - hi.astype(F32)).astype(BF16)
        lo = (xb - hi.astype(F32) - mid.astype(F32)).astype(BF16)
        cb = _dot(ltri, hi) + _dot(ltri, mid) + _dot(ltri, lo) + carry
        cum_ref[blk * tb:(blk + 1) * tb, :] = cb
        cumt_ref[:, blk * tb:(blk + 1) * tb] = cb.T
        carry = cb[tb - 1:tb, :]


def _fox_kernel(q_ref, k_ref, vb_ref, g_ref, cq_ref, ck_ref, o_ref, kn_ref):
    hg = pl.program_id(1)
    i = pl.program_id(2)
    tb = ATT_BLK
    heads = range(ATT_HG)
    hsl = [slice(hh * HD, (hh + 1) * HD) for hh in heads]

    def hnorm(x, g):
        x = x.astype(F32)
        return x * lax.rsqrt(jnp.mean(x * x, axis=-1, keepdims=True) + NORM_EPS) * g

    @pl.when(i == 0)
    def _():
        for sl in hsl:
            kn_ref[:, sl] = hnorm(k_ref[:, sl], g_ref[1:2, :]).astype(BF16)

    q = [(hnorm(q_ref[:, sl], g_ref[0:1, :]) * (HD ** -0.5 * LOG2E)).astype(BF16) for sl in hsl]
    cq_all = cq_ref[...] * LOG2E
    lane = _iota2((tb, LANES), 1)
    cq = [jnp.sum(jnp.where(lane == hg * ATT_HG + hh, cq_all, 0.0), axis=-1, keepdims=True) for hh in heads]
    r, c = _iota2((tb, tb), 0), _iota2((tb, tb), 1)
    causal = c <= r

    def block(j, m_run, l_run, acc, masked):
        rows = pl.ds(pl.multiple_of(j * tb, tb), tb)
        ck8 = ck_ref[:, rows]
        sub = _iota2(ck8.shape, 0)
        ck = [jnp.sum(jnp.where(sub == hg * ATT_HG + hh, ck8, 0.0), axis=0, keepdims=True) * LOG2E for hh in heads]
        s = [_dot_nt(q[hh], kn_ref[rows, hsl[hh]]) - ck[hh] for hh in heads]
        if masked:
            s = [jnp.where(causal, ss, -jnp.inf) for ss in s]
        m_new = [jnp.maximum(m_run[hh], jnp.max(s[hh], axis=-1, keepdims=True) + cq[hh]) for hh in heads]
        alpha = [jnp.exp2(m_run[hh] - m_new[hh]) for hh in heads]
        p = [jnp.exp2(s[hh] - (m_new[hh] - cq[hh])) for hh in heads]
        l_new = [alpha[hh] * l_run[hh] + jnp.sum(p[hh], axis=-1, keepdims=True) for hh in heads]
        acc = [alpha[hh] * acc[hh] + _dot(p[hh].astype(BF16), vb_ref[rows, hsl[hh]]) for hh in heads]
        return m_new, l_new, acc

    st = block(i, [jnp.full((tb, 1), -jnp.inf, F32)] * ATT_HG, [jnp.zeros((tb, 1), F32)] * ATT_HG,
               [jnp.zeros((tb, HD), F32)] * ATT_HG, True)

    def body(n, st):
        return block(i - 1 - n, st[0], st[1], st[2], False)

    m_run, l_run, acc = lax.fori_loop(0, i, body, st)
    for hh in heads:
        o_ref[:, hsl[hh]] = (acc[hh] / l_run[hh]).astype(BF16)


def _fox_mixer(pc, ps, batch, seq_len, qk_g, b_f_pad):
    m = pc.shape[0]
    tb = ATT_BLK
    nq = seq_len // tb
    cum, cumt = pl.pallas_call(
        _fox_cum_kernel,
        out_shape=[jax.ShapeDtypeStruct((m, LANES), F32), jax.ShapeDtypeStruct((batch * LANES, seq_len), F32)],
        grid=(batch,),
        in_specs=[pl.BlockSpec((seq_len, LANES), lambda b: (b, 0)), pl.BlockSpec((1, LANES), lambda b: (0, 0))],
        out_specs=[pl.BlockSpec((seq_len, LANES), lambda b: (b, 0)), pl.BlockSpec((LANES, seq_len), lambda b: (b, 0))],
        compiler_params=_cparams(("parallel",)),
        name="fox_cum",
    )(ps, b_f_pad)
    ng = 1024 // (HD * ATT_HG)
    gw = HD * ATT_HG
    return pl.pallas_call(
        _fox_kernel,
        out_shape=jax.ShapeDtypeStruct((m, 1024), BF16),
        grid=(batch, ng, nq),
        in_specs=[
            pl.BlockSpec((tb, gw), lambda b, h, i: (b * nq + i, h)),
            pl.BlockSpec((seq_len, gw), lambda b, h, i: (b, ng + h)),
            pl.BlockSpec((seq_len, gw), lambda b, h, i: (b, 2 * ng + h)),
            pl.BlockSpec((2, HD), lambda b, h, i: (0, 0)),
            pl.BlockSpec((tb, LANES), lambda b, h, i: (b * nq + i, 0)),
            pl.BlockSpec((SUBLANES, seq_len), lambda b, h, i: (b * (LANES // SUBLANES), 0)),
        ],
        out_specs=pl.BlockSpec((tb, gw), lambda b, h, i: (b * nq + i, h)),
        scratch_shapes=[pltpu.VMEM((seq_len, gw), BF16)],
        compiler_params=_cparams(("parallel", "parallel", "arbitrary")),
        name="forgetting_attention",
    )(pc, pc, pc, qk_g, cum, cumt)


HALO_BF16 = 16


def _dn_pre_kernel(seq_len, p_ref, halo_ref, s_ref, cw_ref, gp_ref, q_ref, k_ref, v_ref, gb_ref):
    tr = p_ref.shape[0]
    w = 1024
    first = (pl.program_id(0) * tr) % seq_len == 0
    x = p_ref[...].astype(F32)
    halo = jnp.where(first, 0.0, halo_ref[HALO_BF16 - SUBLANES:, :].astype(F32))
    acc = x * cw_ref[DN_CONV - 1:DN_CONV, :]
    for s in range(1, DN_CONV):
        acc = acc + _shift_rows(x, halo, s) * cw_ref[DN_CONV - 1 - s:DN_CONV - s, :]
    y = acc * _sigmoid(acc)
    for hh in range(w // HD):
        qs = y[:, hh * HD:(hh + 1) * HD]
        ks = y[:, w + hh * HD:w + (hh + 1) * HD]
        q_ref[:, hh * HD:(hh + 1) * HD] = qs * lax.rsqrt(jnp.sum(qs * qs, -1, keepdims=True) + L2_EPS) * (HD ** -0.5)
        k_ref[:, hh * HD:(hh + 1) * HD] = ks * lax.rsqrt(jnp.sum(ks * ks, -1, keepdims=True) + L2_EPS)
    v_ref[...] = y[:, 2 * w:3 * w]
    s = s_ref[...]
    nh = w // HD
    beta = _sigmoid(s)
    g = -jnp.exp(gp_ref[0:1, :]) * _softplus(s + gp_ref[1:2, :])
    gb_ref[...] = jnp.where(_iota2(s.shape, 1) < nh, beta, g)


def _dn_scan_kernel(q_ref, k_ref, v_ref, z_ref, gb_ref, ng_ref, y_ref, s_ref):
    @pl.when(pl.program_id(1) == 0)
    def _():
        s_ref[...] = jnp.zeros_like(s_ref)

    for cc in range(q_ref.shape[0] // DN_CHUNK):
        _dn_chunk(slice(cc * DN_CHUNK, (cc + 1) * DN_CHUNK), q_ref, k_ref, v_ref, z_ref, gb_ref, ng_ref, y_ref, s_ref)


def _dn_chunk(rs, q_ref, k_ref, v_ref, z_ref, gb_ref, ng_ref, y_ref, s_ref):
    c = DN_CHUNK
    nh = 1024 // HD
    gb = gb_ref[rs, :]
    ltri = jnp.where(_tri(c, strict=False), 1.0, 0.0).astype(BF16)
    hi = gb.astype(BF16)
    mid = (gb - hi.astype(F32)).astype(BF16)
    lo = (gb - hi.astype(F32) - mid.astype(F32)).astype(BF16)
    gc = _dot(ltri, hi) + _dot(ltri, mid) + _dot(ltri, lo)
    gct = gc.T
    strict, incl = _tri(c, strict=True), _tri(c, strict=False)
    heads = range(nh)
    sls = [slice(h * HD, (h + 1) * HD) for h in heads]
    q = [q_ref[rs, sl] for sl in sls]
    k = [k_ref[rs, sl] for sl in sls]
    beta = [gb[:, h:h + 1] for h in heads]
    gcol = [gc[:, nh + h:nh + h + 1] for h in heads]
    glast = [gc[c - 1:c, nh + h:nh + h + 1] for h in heads]
    eg = [jnp.exp(gcol[h]) for h in heads]
    decay = [jnp.exp(jnp.minimum(gcol[h] - gct[nh + h:nh + h + 1, :], 0.0)) for h in heads]
    kb = [k[h] * beta[h] for h in heads]
    gram = [_dot_nt(jnp.concatenate([kb[h].astype(BF16), q[h].astype(BF16)], axis=0), k[h].astype(BF16))
            for h in heads]
    t_inv = _inv_unit_lower_multi([jnp.where(strict, -gram[h][:c] * decay[h], 0.0) for h in heads], 6, refine=True)
    s0 = [s_ref[h] for h in heads]
    ws = [_dot(jnp.concatenate([(kb[h] * eg[h]).astype(BF16), (q[h] * eg[h]).astype(BF16)], axis=0), s0[h].astype(BF16))
          for h in heads]
    vnb = [_dot(t_inv[h].astype(BF16), (v_ref[rs, sls[h]] * beta[h] - ws[h][:c]).astype(BF16)).astype(BF16)
           for h in heads]
    for h in heads:
        attn = jnp.where(incl, gram[h][c:] * decay[h], 0.0)
        o = ws[h][c:] + _dot(attn.astype(BF16), vnb[h])
        o = o * lax.rsqrt(jnp.mean(o * o, axis=-1, keepdims=True) + NORM_EPS) * ng_ref[...]
        z = z_ref[rs, sls[h]].astype(F32)
        y_ref[rs, sls[h]] =(o * (z * _sigmoid(z))).astype(BF16)
    for h in heads:
        s_ref[h] = s0[h] * jnp.exp(glast[h]) + _dot_tn((k[h] * jnp.exp(glast[h] - gcol[h])).astype(BF16), vnb[h])


def _dn_mixer(pd, ps, batch, seq_len, conv_w, gparams, norm_g, tr=256):
    m = pd.shape[0]
    w = 1024
    row = lambda i: (i, 0)
    fix = lambda i: (0, 0)
    wide = jax.ShapeDtypeStruct((m, w), F32)
    q, k, v, gb = pl.pallas_call(
        functools.partial(_dn_pre_kernel, seq_len),
        out_shape=[wide, wide, wide, jax.ShapeDtypeStruct((m, LANES), F32)],
        grid=(m // tr,),
        in_specs=[
            pl.BlockSpec((tr, 3 * w), row),
            pl.BlockSpec((HALO_BF16, 3 * w), lambda i: (jnp.maximum(i * (tr // HALO_BF16) - 1, 0), 0)),
            pl.BlockSpec((tr, LANES), lambda i: (i, 0)),
            pl.BlockSpec((DN_CONV, 3 * w), fix),
            pl.BlockSpec((2, LANES), fix),
        ],
        out_specs=[pl.BlockSpec((tr, w), row)] * 3 + [pl.BlockSpec((tr, LANES), row)],
        compiler_params=_cparams(("parallel",)),
        name="deltanet_pre",
    )(pd, pd, ps, conv_w, gparams)

    c = DN_CHUNK * SCAN_CHUNKS_PER_STEP
    nchunk = seq_len // c
    blk = pl.BlockSpec((c, w), lambda bi, ci: (bi * nchunk + ci, 0))
    return pl.pallas_call(
        _dn_scan_kernel,
        out_shape=jax.ShapeDtypeStruct((m, w), BF16),
        grid=(batch, nchunk),
        in_specs=[blk, blk, blk,
                  pl.BlockSpec((c, w), lambda bi, ci: (bi * nchunk + ci, 3)),
                  pl.BlockSpec((c, LANES), lambda bi, ci: (bi * nchunk + ci, 0)),
                  pl.BlockSpec((1, HD), lambda bi, ci: (0, 0))],
        out_specs=blk,
        scratch_shapes=[pltpu.VMEM((w // HD, HD, HD), F32)],
        compiler_params=_cparams(("parallel", "arbitrary")),
        name="deltanet_scan",
    )(q, k, v, pd, gb, norm_g)


def _merge_kernel(hn_ref, ya_ref, yb_ref, yc_ref, yd_ref, wg_ref, bg_ref, wb_ref, o_ref, acc_ref):
    i = pl.program_id(1)

    @pl.when(i == 0)
    def _():
        acc_ref[...] = jnp.zeros_like(acc_ref)

    gate = _sigmoid(_dot(hn_ref[...], wg_ref[...]) + bg_ref[...])
    ys = (ya_ref, yb_ref, yc_ref, yd_ref)
    for n in range(4):
        @pl.when(i == n)
        def _(n=n):
            acc_ref[...] += gate * _dot(ys[n][...], wb_ref[...])

    @pl.when(i == 3)
    def _():
        o_ref[...] = acc_ref[...].astype(BF16)


def _outproj_kernel(x_ref, m_ref, w_ref, g_ref, o_ref):
    o_ref[...] = x_ref[...] + _rms(_dot(m_ref[...], w_ref[...]), g_ref[...])


def _merge(x, hn, g3, ys, w_gate, b_gate, w_branch, w_out, l, tm=512):
    m, d = x.shape
    bw = ys[0].shape[1]
    row = lambda r, i: (r, 0)
    merged = pl.pallas_call(
        _merge_kernel,
        out_shape=jax.ShapeDtypeStruct((m, d), BF16),
        grid=(m // tm, 4),
        in_specs=[
            pl.BlockSpec((tm, d), row),
            pl.BlockSpec((tm, bw), row),
            pl.BlockSpec((tm, bw), row),
            pl.BlockSpec((tm, bw), row),
            pl.BlockSpec((tm, bw), row),
            pl.BlockSpec((None, None, d, d), lambda r, i: (l, i, 0, 0)),
            pl.BlockSpec((None, None, 1, d), lambda r, i: (l, i, 0, 0)),
            pl.BlockSpec((None, None, bw, d), lambda r, i: (l, i, 0, 0)),
        ],
        out_specs=pl.BlockSpec((tm, d), row),
        scratch_shapes=[pltpu.VMEM((tm, d), F32)],
        compiler_params=_cparams(("parallel", "arbitrary")),
        name="merge",
    )(hn, ys[0], ys[1], ys[2], ys[3], w_gate, b_gate, w_branch)
    return pl.pallas_call(
        _outproj_kernel,
        out_shape=jax.ShapeDtypeStruct((m, d), F32),
        grid=(m // tm,),
        in_specs=[
            pl.BlockSpec((tm, d), lambda r: (r, 0)),
            pl.BlockSpec((tm, d), lambda r: (r, 0)),
            pl.BlockSpec((None, d, d), lambda r: (l, 0, 0)),
            pl.BlockSpec((1, d), lambda r: (0, 0)),
        ],
        out_specs=pl.BlockSpec((tm, d), lambda r: (r, 0)),
        compiler_params=_cparams(("parallel",)),
        name="outproj",
    )(x, merged, w_out, g3)


def _pad_cols(w, n):
    return jnp.pad(w, ((0, 0), (0, n - w.shape[1])))


def _pad_rows(w, n):
    return jnp.pad(w, ((0, n - w.shape[0]), (0, 0)))


def kernel(x, norm_g, ffn_w_in, ffn_w_out, w_in, w_gate, b_gate, w_branch, w_out, rwkv_mu, rwkv_vec, rwkv_r_k,
           rwkv_w2, rwkv_a2, rwkv_g2, fox_qk_g, fox_b_f, dn_conv, dn_A_log, dn_dt_bias, dn_norm_g):
    batch, seq_len, d = x.shape
    depth = norm_g.shape[0]
    bw = w_branch.shape[2]
    rw_lo = rwkv_w2.shape[1]
    ra_lo = rwkv_a2.shape[1]
    rg_lo = rwkv_g2.shape[1]
    nh = bw // HD
    assert bw == 1024 and rw_lo <= LANES and ra_lo <= LANES and rw_lo + ra_lo >= LANES and rg_lo == 2 * LANES
    assert seq_len % (DN_CHUNK * SCAN_CHUNKS_PER_STEP) == 0 and seq_len % ATT_BLK == 0 and (batch * seq_len) % 1024 == 0

    rw_cols = 3 * bw + rw_lo + ra_lo + rg_lo
    sb_cols = 3 * bw
    fx_cols = 3 * bw + nh
    o_a, o_b, o_c, o_d = 0, rw_cols, rw_cols + sb_cols, rw_cols + sb_cols + fx_cols

    head_of = jnp.arange(bw) // RW_HD
    e_red = (head_of[:, None] == jnp.arange(LANES)[None, :]).astype(BF16)
    e_exp = e_red.T

    w_in_b = w_in.astype(BF16)
    w_gate_b, w_branch_b, w_out_b = w_gate.astype(BF16), w_branch.astype(BF16), w_out.astype(BF16)

    xf = x.reshape(batch * seq_len, d)
    for l in range(depth):
        g = norm_g[l]
        mu_p = jnp.pad(rwkv_mu[l], (0, RW_PA - rw_cols))[None, :]
        w2p = _pad_rows(rwkv_w2[l], LANES)
        a2p = _pad_rows(rwkv_a2[l], LANES)
        b_f_pad = jnp.pad(fox_b_f[l], (0, LANES - nh))[None, :]
        gparams = jnp.stack([jnp.pad(dn_A_log[l], (nh, LANES - 2 * nh)), jnp.pad(dn_dt_bias[l], (nh, LANES - 2 * nh))])

        xf = _ffn(xf, g[0:1], g[1:2], ffn_w_in, ffn_w_out, l, 0)

        hn = _norm(xf, g[2:3])
        pa = _proj(hn, w_in_b, l, o_a, rw_cols, RW_PA // 4, F32, name="proj_rwkv")
        pb = _proj(hn, w_in_b, l, o_b, 3 * bw, 1024, BF16, name="proj_sb")
        pc = _proj(hn, w_in_b, l, o_c, 3 * bw, 1024, BF16, name="proj_fox")
        pd = _proj(hn, w_in_b, l, o_d, 4 * bw, 1024, BF16, name="proj_dn")
        psf = _proj(hn, w_in_b, l, o_c + 3 * bw, nh, LANES, F32, name="proj_fgate")
        psd = _proj(hn, w_in_b, l, o_d + 4 * bw, 2 * nh, LANES, F32, name="proj_dgate")

        ya = _rwkv_mixer(pa, batch, seq_len, rw_lo, ra_lo, mu_p, rwkv_vec[l], rwkv_r_k[l].reshape(1, bw), w2p, a2p,
                         rwkv_g2[l], e_red, e_exp)
        yb = _sb_mixer(pb, batch, seq_len)
        yc = _fox_mixer(pc, psf, batch, seq_len, fox_qk_g[l], b_f_pad)
        yd = _dn_mixer(pd, psd, batch, seq_len, dn_conv[l], gparams, dn_norm_g[l][None, :])

        xf = _merge(xf, hn, g[3:4], (ya, yb, yc, yd), w_gate_b, b_gate[:, :, None, :], w_branch_b, w_out_b, l)

        xf = _ffn(xf, g[4:5], g[5:6], ffn_w_in, ffn_w_out, l, 1)
    return xf.reshape(batch, seq_len, d)
```

```python
import functools

import jax
import jax.numpy as jnp
from jax import lax
from jax.experimental import pallas as pl
from jax.experimental.pallas import tpu as pltpu

F32 = jnp.float32
BF16 = jnp.bfloat16

NORM_EPS = 1e-6
RW_GN_EPS = 64e-5
L2_EPS = 1e-12
LOG2E = 1.4426950408889634

LANES = 128
SUBLANES = 8
VMEM_LIMIT = 58 * 1024 * 1024

RW_HD = 64
RW_CHUNK = 64
HD = 128
DN_CHUNK = 128
DN_CONV = 4
ATT_BLK = 256
ATT_HG = 4
SCAN_CHUNKS_PER_STEP = 4


def _cparams(sem):
    return pltpu.CompilerParams(dimension_semantics=sem, vmem_limit_bytes=VMEM_LIMIT)


def _dot(a, b):
    return jnp.dot(a, b, preferred_element_type=F32)


def _dot_nt(a, b):
    return lax.dot_general(a, b, (((1,), (1,)), ((), ())), preferred_element_type=F32)


def _dot_tn(a, b):
    return lax.dot_general(a, b, (((0,), (0,)), ((), ())), preferred_element_type=F32)


def _split(x):
    hi = x.astype(BF16)
    lo = (x - hi.astype(F32)).astype(BF16)
    return hi, lo


def _dot_x2(a_exact, x):
    hi, lo = _split(x)
    return _dot(a_exact, hi) + _dot(a_exact, lo)


def _dot_2x(x, b_exact):
    hi, lo = _split(x)
    return _dot(hi, b_exact) + _dot(lo, b_exact)


def _headsum(x, e_red, e_exp):
    return _dot_2x(_dot_2x(x, e_red), e_exp)


def _dot_3(a, b):
    ah, al = _split(a)
    bh, bl = _split(b)
    return _dot(ah, bh) + _dot(ah, bl) + _dot(al, bh)


def _sigmoid(x):
    return 1.0 / (1.0 + jnp.exp(-x))


def _softplus(x):
    return jnp.maximum(x, 0.0) + jnp.log(1.0 + jnp.exp(-jnp.abs(x)))


def _rms(x, g):
    return x * lax.rsqrt(jnp.mean(x * x, axis=-1, keepdims=True) + NORM_EPS) * g


def _iota2(shape, dim):
    return lax.broadcasted_iota(jnp.int32, shape, dim)


def _tri(n, strict):
    r, c = _iota2((n, n), 0), _iota2((n, n), 1)
    return (c < r) if strict else (c <= r)


def _shift_rows(x, halo, k):
    rolled = pltpu.roll(x, k, axis=0)
    hx = pltpu.roll(halo, k, axis=0)
    top = jnp.where(_iota2(hx.shape, 0) < k, hx, rolled[0:SUBLANES])
    return jnp.concatenate([top, rolled[SUBLANES:]], axis=0)


def _inv_unit_lower_multi(n_mats, steps, refine):
    n = n_mats[0].shape[0]
    eye = jnp.where(_iota2((n, n), 0) == _iota2((n, n), 1), 1.0, 0.0)
    ps = [eye + nm for nm in n_mats]
    ms = list(n_mats)
    for _ in range(steps):
        mbs = [m.astype(BF16) for m in ms]
        ms = [_dot(mb, mb) for mb in mbs]
        ps = [p + _dot(p.astype(BF16), m.astype(BF16)) for p, m in zip(ps, ms)]
    if not refine:
        return ps
    res = [(eye - p) + _dot_3(nm, p) for nm, p in zip(n_mats, ps)]
    return [p + _dot(p.astype(BF16), r.astype(BF16)) for p, r in zip(ps, res)]


def _ffn_kernel(x_ref, g0_ref, g1_ref, wg_ref, wu_ref, wo_ref, o_ref, xn_ref):
    j = pl.program_id(1)

    @pl.when(j == 0)
    def _():
        xn_ref[...] = _rms(x_ref[...], g0_ref[...]).astype(BF16)
        o_ref[...] = jnp.zeros_like(o_ref)

    xn = xn_ref[...]
    gate = _dot(xn, wg_ref[...].astype(BF16))
    up = _dot(xn, wu_ref[...].astype(BF16))
    act = (gate * _sigmoid(gate) * up).astype(BF16)
    o_ref[...] += _dot(act, wo_ref[...].astype(BF16))

    @pl.when(j == pl.num_programs(1) - 1)
    def _():
        o_ref[...] = x_ref[...] + 0.5 * _rms(o_ref[...], g1_ref[...])


def _ffn(x, g0, g1, w_in, w_out, l, k, tm=1024, tf=256):
    m, d = x.shape
    dff = w_out.shape[2]
    nf = dff // tf
    return pl.pallas_call(
        _ffn_kernel,
        out_shape=jax.ShapeDtypeStruct((m, d), F32),
        grid=(m // tm, nf),
        in_specs=[
            pl.BlockSpec((tm, d), lambda i, j: (i, 0)),
            pl.BlockSpec((1, d), lambda i, j: (0, 0)),
            pl.BlockSpec((1, d), lambda i, j: (0, 0)),
            pl.BlockSpec((None, None, d, tf), lambda i, j: (l, k, 0, j)),
            pl.BlockSpec((None, None, d, tf), lambda i, j: (l, k, 0, j + nf)),
            pl.BlockSpec((None, None, tf, d), lambda i, j: (l, k, j, 0)),
        ],
        out_specs=pl.BlockSpec((tm, d), lambda i, j: (i, 0)),
        scratch_shapes=[pltpu.VMEM((tm, d), BF16)],
        compiler_params=_cparams(("parallel", "arbitrary")),
        name="ffn",
    )(x, g0, g1, w_in, w_in, w_out)


def _norm_kernel(x_ref, g_ref, o_ref):
    o_ref[...] = _rms(x_ref[...], g_ref[...]).astype(BF16)


def _norm(x, g, tm=512):
    m, d = x.shape
    return pl.pallas_call(
        _norm_kernel,
        out_shape=jax.ShapeDtypeStruct((m, d), BF16),
        grid=(m // tm,),
        in_specs=[pl.BlockSpec((tm, d), lambda i: (i, 0)), pl.BlockSpec((1, d), lambda i: (0, 0))],
        out_specs=pl.BlockSpec((tm, d), lambda i: (i, 0)),
        compiler_params=_cparams(("parallel",)),
        name="norm",
    )(x, g)


def _proj_kernel(shift, width, two_blocks, h_ref, *refs):
    if two_blocks:
        w0_ref, w1_ref, o_ref, ws_ref = refs
    else:
        w0_ref, o_ref, ws_ref = refs
    j, i = pl.program_id(0), pl.program_id(1)
    tn = o_ref.shape[1]

    @pl.when(i == 0)
    def _():
        w = w0_ref[...]
        if two_blocks:
            w = jnp.concatenate([w, w1_ref[...]], axis=1)
        u = pltpu.bitcast(w, jnp.uint32)
        if shift:
            u = pltpu.roll(u, u.shape[1] - shift, axis=1)
        u = u[:, :tn]
        if width % tn:
            u = jnp.where(j * tn + _iota2(u.shape, 1) < width, u, jnp.uint32(0))
        ws_ref[...] = pltpu.bitcast(u, BF16)

    o_ref[...] = _dot(h_ref[...], ws_ref[...]).astype(o_ref.dtype)


def _proj(hn, w_all, l, off, width, tn, out_dtype, tm=1024, name="proj"):
    m, d = hn.shape
    c0, shift = off // tn, off % tn
    nt = -(-width // tn)
    two_blocks = shift > 0 and shift + min(width, tn) > tn
    w_specs = [pl.BlockSpec((None, d, tn), lambda j, i: (l, 0, c0 + j))]
    if two_blocks:
        w_specs.append(pl.BlockSpec((None, d, tn), lambda j, i: (l, 0, c0 + j + 1)))
    return pl.pallas_call(
        functools.partial(_proj_kernel, shift, width, two_blocks),
        out_shape=jax.ShapeDtypeStruct((m, nt * tn), out_dtype),
        grid=(nt, m // tm),
        in_specs=[pl.BlockSpec((tm, d), lambda j, i: (i, 0))] + w_specs,
        out_specs=pl.BlockSpec((tm, tn), lambda j, i: (i, j)),
        scratch_shapes=[pltpu.VMEM((d, tn), BF16)],
        compiler_params=_cparams(("parallel", "arbitrary")),
        name=name,
    )(hn, *([w_all] * len(w_specs)))


RW_W = 1024
RW_PA = 3584


def _rwkv_pre_kernel(seq_len, rw_lo, ra_lo, p_ref, halo_ref, mu_ref, vec_ref, rk_ref, w2_ref, a2_ref, g2_ref,
                     er_ref, ex_ref, r_ref, lw_ref, k_ref, v_ref, nkk_ref, b_ref, g_ref, bonus_ref):
    tr = p_ref.shape[0]
    x = p_ref[...]
    first = (pl.program_id(0) * tr) % seq_len == 0
    halo = jnp.where(first, 0.0, halo_ref[...])
    prev = _shift_rows(x, halo, 1)
    x = x + (prev - x) * mu_ref[...]
    w = RW_W
    r, k, v = x[:, 0:w], x[:, w:2 * w], x[:, 2 * w:3 * w]
    tail = x[:, 3 * w:]
    lane = _iota2((tr, LANES), 1)
    w_lo = jnp.where(lane < rw_lo, tail[:, 0:LANES], 0.0)
    a_lo = jnp.where(lane < ra_lo, pltpu.roll(tail[:, 0:2 * LANES], 2 * LANES - rw_lo, axis=1)[:, 0:LANES], 0.0)
    g_off = rw_lo + ra_lo - LANES
    g_lo = pltpu.roll(tail[:, LANES:], 3 * LANES - g_off, axis=1)[:, 0:2 * LANES]
    w0, a0, k_k, k_a = vec_ref[0:1, :], vec_ref[1:2, :], vec_ref[2:3, :], vec_ref[3:4, :]
    lw = -jnp.exp(-0.5) * _sigmoid(w0 + _dot_3(jnp.tanh(w_lo), w2_ref[...]))
    a = _sigmoid(a0 + _dot_3(a_lo, a2_ref[...]))
    g = _dot_3(_sigmoid(g_lo), g2_ref[...])
    e_red, e_exp = er_ref[...], ex_ref[...]
    kk = k * k_k
    kk = kk * lax.rsqrt(_headsum(kk * kk, e_red, e_exp) + L2_EPS)
    k = k * (1.0 + (a - 1.0) * k_a)
    bonus = _headsum(r * k * rk_ref[...], e_red, e_exp) * v
    r_ref[...] = r
    lw_ref[...] = lw
    k_ref[...] = k
    v_ref[...] = v
    nkk_ref[...] = -kk
    b_ref[...] = kk * a
    g_ref[...] = g
    bonus_ref[...] = bonus


def _rwkv_scan_kernel(r_ref, lw_ref, k_ref, v_ref, nkk_ref, b_ref, o_ref, s_ref):
    @pl.when(pl.program_id(1) == 0)
    def _():
        s_ref[...] = jnp.zeros_like(s_ref)

    for cc in range(r_ref.shape[0] // RW_CHUNK):
        _rwkv_chunk(slice(cc * RW_CHUNK, (cc + 1) * RW_CHUNK), r_ref, lw_ref, k_ref, v_ref, nkk_ref, b_ref, o_ref, s_ref)


def _rwkv_chunk(rs, r_ref, lw_ref, k_ref, v_ref, nkk_ref, b_ref, o_ref, s_ref):
    c = RW_CHUNK
    lw = lw_ref[rs, :]
    ltri = jnp.where(_tri(c, strict=False), 1.0, 0.0).astype(BF16)
    cum = _dot_x2(ltri, lw)
    tot = cum[c - 1:c, :]
    e_in, e_neg = jnp.exp(cum), jnp.exp(-cum)
    e_ex, e_end = jnp.exp(cum - lw), jnp.exp(tot - cum)
    r, k, v, nkk, b = r_ref[rs, :], k_ref[rs, :], v_ref[rs, :], nkk_ref[rs, :], b_ref[rs, :]
    rt, kt, bt, at = r * e_in, k * e_neg, b * e_neg, nkk * e_ex
    kh, bh = k * e_end, b * e_end
    e_tot = jnp.exp(tot)

    n2 = 2 * c
    lane_lo = _iota2((c, LANES), 1) < RW_HD
    ri, ci = _iota2((n2, n2), 0) & (c - 1), _iota2((n2, n2), 1) & (c - 1)
    strict, incl = ci < ri, ci <= ri

    def stack(xp):
        return jnp.concatenate([jnp.where(lane_lo, xp, 0.0), jnp.where(lane_lo, 0.0, xp)], axis=0).astype(BF16)

    pairs = range(RW_W // LANES)
    sls = [slice(pr * LANES, (pr + 1) * LANES) for pr in pairs]
    at_s = [stack(at[:, sl]) for sl in sls]
    rt_s = [stack(rt[:, sl]) for sl in sls]
    bt_s = [stack(bt[:, sl]) for sl in sls]
    kt_s = [stack(kt[:, sl]) for sl in sls]
    v_s = [stack(v[:, sl]) for sl in sls]
    gram = [_dot_nt(jnp.concatenate([at_s[p], rt_s[p]], axis=0), jnp.concatenate([bt_s[p], kt_s[p]], axis=0))
            for p in pairs]
    t_inv = _inv_unit_lower_multi([jnp.where(strict, gm[:n2, :n2], 0.0) for gm in gram], 5, refine=False)
    akv = [_dot(jnp.where(strict, gram[p][:n2, n2:], 0.0).astype(BF16), v_s[p]) for p in pairs]
    s0 = [s_ref[p] for p in pairs]
    ws = [_dot_nt(jnp.concatenate([at_s[p], rt_s[p]], axis=0), s0[p].astype(BF16)) for p in pairs]
    u = [_dot(t_inv[p].astype(BF16), (ws[p][:n2] + akv[p]).astype(BF16)) for p in pairs]
    uv = [jnp.concatenate([u[p].astype(BF16), v_s[p]], axis=0) for p in pairs]
    for p in pairs:
        a_r = jnp.concatenate([jnp.where(incl, gram[p][n2:, :n2], 0.0), jnp.where(incl, gram[p][n2:, n2:], 0.0)], axis=1)
        o_s = ws[p][n2:] + _dot(a_r.astype(BF16), uv[p])
        o_ref[rs, sls[p]] = o_s[:c] + o_s[c:]
    for p in pairs:
        bk = jnp.concatenate([stack(bh[:, sls[p]]), stack(kh[:, sls[p]])], axis=0)
        s_ref[p] = s0[p] * e_tot[:, sls[p]] + _dot_tn(uv[p], bk)


def _rwkv_post_kernel(o_ref, bonus_ref, g_ref, ln_ref, er_ref, ex_ref, y_ref):
    o = o_ref[...]
    e_red, e_exp = er_ref[...], ex_ref[...]
    mean = _headsum(o, e_red, e_exp) * (1.0 / RW_HD)
    d = o - mean
    var = _headsum(d * d, e_red, e_exp) * (1.0 / RW_HD)
    o = d * lax.rsqrt(var + RW_GN_EPS) * ln_ref[0:1, :] + ln_ref[1:2, :]
    y_ref[...] = ((o + bonus_ref[...]) * g_ref[...]).astype(BF16)


def _rwkv_mixer(pa, batch, seq_len, rw_lo, ra_lo, mu, vec, r_k, w2, a2, g2, e_red, e_exp, tr=256):
    m = pa.shape[0]
    w = RW_W
    row = lambda i: (i, 0)
    fix = lambda i: (0, 0)
    wide = jax.ShapeDtypeStruct((m, w), F32)
    r, lw, k, v, nkk, b, g, bonus = pl.pallas_call(
        functools.partial(_rwkv_pre_kernel, seq_len, rw_lo, ra_lo),
        out_shape=[wide] * 8,
        grid=(m // tr,),
        in_specs=[
            pl.BlockSpec((tr, RW_PA), row),
            pl.BlockSpec((SUBLANES, RW_PA), lambda i: (jnp.maximum(i * (tr // SUBLANES) - 1, 0), 0)),
            pl.BlockSpec((1, RW_PA), fix),
            pl.BlockSpec((4, w), fix),
            pl.BlockSpec((1, w), fix),
            pl.BlockSpec((LANES, w), fix),
            pl.BlockSpec((LANES, w), fix),
            pl.BlockSpec((2 * LANES, w), fix),
            pl.BlockSpec((w, LANES), fix),
            pl.BlockSpec((LANES, w), fix),
        ],
        out_specs=[pl.BlockSpec((tr, w), row)] * 8,
        compiler_params=_cparams(("parallel",)),
        name="rwkv_pre",
    )(pa, pa, mu, vec[0:4], r_k, w2, a2, g2, e_red, e_exp)

    c = RW_CHUNK * SCAN_CHUNKS_PER_STEP
    nchunk = seq_len // c
    blk = pl.BlockSpec((c, w), lambda bi, ci: (bi * nchunk + ci, 0))
    o = pl.pallas_call(
        _rwkv_scan_kernel,
        out_shape=wide,
        grid=(batch, nchunk),
        in_specs=[blk] * 6,
        out_specs=blk,
        scratch_shapes=[pltpu.VMEM((w // LANES, LANES, LANES), F32)],
        compiler_params=_cparams(("parallel", "arbitrary")),
        name="rwkv_scan",
    )(r, lw, k, v, nkk, b)

    return pl.pallas_call(
        _rwkv_post_kernel,
        out_shape=jax.ShapeDtypeStruct((m, w), BF16),
        grid=(m // tr,),
        in_specs=[pl.BlockSpec((tr, w), row)] * 3 + [pl.BlockSpec((2, w), fix), pl.BlockSpec((w, LANES), fix),
                  pl.BlockSpec((LANES, w), fix)],
        out_specs=pl.BlockSpec((tr, w), row),
        compiler_params=_cparams(("parallel",)),
        name="rwkv_post",
    )(o, bonus, g, vec[4:6], e_red, e_exp)


def _sb_kernel(q_ref, kb_ref, vb_ref, o_ref):
    i = pl.program_id(2)
    tb = ATT_BLK
    heads = range(ATT_HG)
    hsl = [slice(hh * HD, (hh + 1) * HD) for hh in heads]
    q = [(q_ref[:, sl].astype(F32) * (HD ** -0.5 * LOG2E)).astype(BF16) for sl in hsl]
    r, c = _iota2((tb, tb), 0), _iota2((tb, tb), 1)
    upper = jnp.where(r > c, 1.0, 0.0).astype(BF16)
    diag_ok = c < r

    def block(j, carry, acc, masked):
        rows = pl.ds(pl.multiple_of(j * tb, tb), tb)
        z = [_dot_nt(q[hh], kb_ref[rows, hsl[hh]]) for hh in heads]
        nz = [-zz for zz in z]
        l1 = [jnp.log2(1.0 + jnp.exp2(jnp.minimum(z[hh], nz[hh]))) for hh in heads]
        lk = [jnp.minimum(nz[hh], 0.0) - l1[hh] for hh in heads]
        if masked:
            lk = [jnp.where(diag_ok, x, 0.0) for x in lk]
        later = [_dot(lk[hh].astype(BF16), upper) + carry[hh] for hh in heads]
        a = [jnp.exp2((jnp.minimum(z[hh], 0.0) - l1[hh]) + later[hh]) for hh in heads]
        if masked:
            a = [jnp.where(diag_ok, aa, 0.0) for aa in a]
        acc = [acc[hh] + _dot(a[hh].astype(BF16), vb_ref[rows, hsl[hh]]) for hh in heads]
        carry = [later[hh][:, 0:1] + lk[hh][:, 0:1] for hh in heads]
        return carry, acc

    st = block(i, [jnp.zeros((tb, 1), F32)] * ATT_HG, [jnp.zeros((tb, HD), F32)] * ATT_HG, True)

    def body(n, st):
        return block(i - 1 - n, st[0], st[1], False)

    carry, acc = lax.fori_loop(0, i, body, st)
    for hh in heads:
        o_ref[:, hsl[hh]] = acc[hh].astype(BF16)


def _sb_mixer(pb, batch, seq_len):
    m = pb.shape[0]
    tb = ATT_BLK
    nq = seq_len // tb
    ng = 1024 // (HD * ATT_HG)
    gw = HD * ATT_HG
    return pl.pallas_call(
        _sb_kernel,
        out_shape=jax.ShapeDtypeStruct((m, 1024), BF16),
        grid=(batch, ng, nq),
        in_specs=[
            pl.BlockSpec((tb, gw), lambda b, h, i: (b * nq + i, h)),
            pl.BlockSpec((seq_len, gw), lambda b, h, i: (b, ng + h)),
            pl.BlockSpec((seq_len, gw), lambda b, h, i: (b, 2 * ng + h)),
        ],
        out_specs=pl.BlockSpec((tb, gw), lambda b, h, i: (b * nq + i, h)),
        compiler_params=_cparams(("parallel", "parallel", "arbitrary")),
        name="stick_breaking",
    )(pb, pb, pb)


def _fox_cum_kernel(f_ref, bf_ref, cum_ref, cumt_ref):
    tb = LANES
    x = f_ref[...] + bf_ref[...]
    lf = jnp.minimum(x, 0.0) - jnp.log(1.0 + jnp.exp(-jnp.abs(x)))
    ltri = jnp.where(_tri(tb, strict=False), 1.0, 0.0).astype(BF16)
    carry = jnp.zeros((1, LANES), F32)
    for blk in range(f_ref.shape[0] // tb):
        xb = lf[blk * tb:(blk + 1) * tb, :]
        hi = xb.astype(BF16)
        mid = (xb - hi.astype(F32)).astype(BF16)
        lo = (xb - hi.astype(F32) - mid.astype(F32)).astype(BF16)
        cb = _dot(ltri, hi) + _dot(ltri, mid) + _dot(ltri, lo) + carry
        cum_ref[blk * tb:(blk + 1) * tb, :] = cb
        cumt_ref[:, blk * tb:(blk + 1) * tb] = cb.T
        carry = cb[tb - 1:tb, :]


def _fox_kernel(q_ref, k_ref, vb_ref, g_ref, cq_ref, ck_ref, o_ref, kn_ref):
    hg = pl.program_id(1)
    i = pl.program_id(2)
    tb = ATT_BLK
    heads = range(ATT_HG)
    hsl = [slice(hh * HD, (hh + 1) * HD) for hh in heads]

    def hnorm(x, g):
        x = x.astype(F32)
        return x * lax.rsqrt(jnp.mean(x * x, axis=-1, keepdims=True) + NORM_EPS) * g

    @pl.when(i == 0)
    def _():
        for sl in hsl:
            kn_ref[:, sl] = hnorm(k_ref[:, sl], g_ref[1:2, :]).astype(BF16)

    q = [(hnorm(q_ref[:, sl], g_ref[0:1, :]) * (HD ** -0.5 * LOG2E)).astype(BF16) for sl in hsl]
    cq_all = cq_ref[...] * LOG2E
    lane = _iota2((tb, LANES), 1)
    cq = [jnp.sum(jnp.where(lane == hg * ATT_HG + hh, cq_all, 0.0), axis=-1, keepdims=True) for hh in heads]
    r, c = _iota2((tb, tb), 0), _iota2((tb, tb), 1)
    causal = c <= r

    def block(j, m_run, l_run, acc, masked):
        rows = pl.ds(pl.multiple_of(j * tb, tb), tb)
        ck8 = ck_ref[:, rows]
        sub = _iota2(ck8.shape, 0)
        ck = [jnp.sum(jnp.where(sub == hg * ATT_HG + hh, ck8, 0.0), axis=0, keepdims=True) * LOG2E for hh in heads]
        s = [_dot_nt(q[hh], kn_ref[rows, hsl[hh]]) - ck[hh] for hh in heads]
        if masked:
            s = [jnp.where(causal, ss, -jnp.inf) for ss in s]
        m_new = [jnp.maximum(m_run[hh], jnp.max(s[hh], axis=-1, keepdims=True) + cq[hh]) for hh in heads]
        alpha = [jnp.exp2(m_run[hh] - m_new[hh]) for hh in heads]
        p = [jnp.exp2(s[hh] - (m_new[hh] - cq[hh])) for hh in heads]
        l_new = [alpha[hh] * l_run[hh] + jnp.sum(p[hh], axis=-1, keepdims=True) for hh in heads]
        acc = [alpha[hh] * acc[hh] + _dot(p[hh].astype(BF16), vb_ref[rows, hsl[hh]]) for hh in heads]
        return m_new, l_new, acc

    st = block(i, [jnp.full((tb, 1), -jnp.inf, F32)] * ATT_HG, [jnp.zeros((tb, 1), F32)] * ATT_HG,
               [jnp.zeros((tb, HD), F32)] * ATT_HG, True)

    def body(n, st):
        return block(i - 1 - n, st[0], st[1], st[2], False)

    m_run, l_run, acc = lax.fori_loop(0, i, body, st)
    for hh in heads:
        o_ref[:, hsl[hh]] = (acc[hh] / l_run[hh]).astype(BF16)


def _fox_mixer(pc, ps, batch, seq_len, qk_g, b_f_pad):
    m = pc.shape[0]
    tb = ATT_BLK
    nq = seq_len // tb
    cum, cumt = pl.pallas_call(
        _fox_cum_kernel,
        out_shape=[jax.ShapeDtypeStruct((m, LANES), F32), jax.ShapeDtypeStruct((batch * LANES, seq_len), F32)],
        grid=(batch,),
        in_specs=[pl.BlockSpec((seq_len, LANES), lambda b: (b, 0)), pl.BlockSpec((1, LANES), lambda b: (0, 0))],
        out_specs=[pl.BlockSpec((seq_len, LANES), lambda b: (b, 0)), pl.BlockSpec((LANES, seq_len), lambda b: (b, 0))],
        compiler_params=_cparams(("parallel",)),
        name="fox_cum",
    )(ps, b_f_pad)
    ng = 1024 // (HD * ATT_HG)
    gw = HD * ATT_HG
    return pl.pallas_call(
        _fox_kernel,
        out_shape=jax.ShapeDtypeStruct((m, 1024), BF16),
        grid=(batch, ng, nq),
        in_specs=[
            pl.BlockSpec((tb, gw), lambda b, h, i: (b * nq + i, h)),
            pl.BlockSpec((seq_len, gw), lambda b, h, i: (b, ng + h)),
            pl.BlockSpec((seq_len, gw), lambda b, h, i: (b, 2 * ng + h)),
            pl.BlockSpec((2, HD), lambda b, h, i: (0, 0)),
            pl.BlockSpec((tb, LANES), lambda b, h, i: (b * nq + i, 0)),
            pl.BlockSpec((SUBLANES, seq_len), lambda b, h, i: (b * (LANES // SUBLANES), 0)),
        ],
        out_specs=pl.BlockSpec((tb, gw), lambda b, h, i: (b * nq + i, h)),
        scratch_shapes=[pltpu.VMEM((seq_len, gw), BF16)],
        compiler_params=_cparams(("parallel", "parallel", "arbitrary")),
        name="forgetting_attention",
    )(pc, pc, pc, qk_g, cum, cumt)


HALO_BF16 = 16


def _dn_pre_kernel(seq_len, p_ref, halo_ref, s_ref, cw_ref, gp_ref, q_ref, k_ref, v_ref, gb_ref):
    tr = p_ref.shape[0]
    w = 1024
    first = (pl.program_id(0) * tr) % seq_len == 0
    x = p_ref[...].astype(F32)
    halo = jnp.where(first, 0.0, halo_ref[HALO_BF16 - SUBLANES:, :].astype(F32))
    acc = x * cw_ref[DN_CONV - 1:DN_CONV, :]
    for s in range(1, DN_CONV):
        acc = acc + _shift_rows(x, halo, s) * cw_ref[DN_CONV - 1 - s:DN_CONV - s, :]
    y = acc * _sigmoid(acc)
    for hh in range(w // HD):
        qs = y[:, hh * HD:(hh + 1) * HD]
        ks = y[:, w + hh * HD:w + (hh + 1) * HD]
        q_ref[:, hh * HD:(hh + 1) * HD] = qs * lax.rsqrt(jnp.sum(qs * qs, -1, keepdims=True) + L2_EPS) * (HD ** -0.5)
        k_ref[:, hh * HD:(hh + 1) * HD] = ks * lax.rsqrt(jnp.sum(ks * ks, -1, keepdims=True) + L2_EPS)
    v_ref[...] = y[:, 2 * w:3 * w]
    s = s_ref[...]
    nh = w // HD
    beta = _sigmoid(s)
    g = -jnp.exp(gp_ref[0:1, :]) * _softplus(s + gp_ref[1:2, :])
    gb_ref[...] = jnp.where(_iota2(s.shape, 1) < nh, beta, g)


def _dn_scan_kernel(q_ref, k_ref, v_ref, z_ref, gb_ref, ng_ref, y_ref, s_ref):
    @pl.when(pl.program_id(1) == 0)
    def _():
        s_ref[...] = jnp.zeros_like(s_ref)

    for cc in range(q_ref.shape[0] // DN_CHUNK):
        _dn_chunk(slice(cc * DN_CHUNK, (cc + 1) * DN_CHUNK), q_ref, k_ref, v_ref, z_ref, gb_ref, ng_ref, y_ref, s_ref)


def _dn_chunk(rs, q_ref, k_ref, v_ref, z_ref, gb_ref, ng_ref, y_ref, s_ref):
    c = DN_CHUNK
    nh = 1024 // HD
    gb = gb_ref[rs, :]
    ltri = jnp.where(_tri(c, strict=False), 1.0, 0.0).astype(BF16)
    hi = gb.astype(BF16)
    mid = (gb - hi.astype(F32)).astype(BF16)
    lo = (gb - hi.astype(F32) - mid.astype(F32)).astype(BF16)
    gc = _dot(ltri, hi) + _dot(ltri, mid) + _dot(ltri, lo)
    gct = gc.T
    strict, incl = _tri(c, strict=True), _tri(c, strict=False)
    heads = range(nh)
    sls = [slice(h * HD, (h + 1) * HD) for h in heads]
    q = [q_ref[rs, sl] for sl in sls]
    k = [k_ref[rs, sl] for sl in sls]
    beta = [gb[:, h:h + 1] for h in heads]
    gcol = [gc[:, nh + h:nh + h + 1] for h in heads]
    glast = [gc[c - 1:c, nh + h:nh + h + 1] for h in heads]
    eg = [jnp.exp(gcol[h]) for h in heads]
    decay = [jnp.exp(jnp.minimum(gcol[h] - gct[nh + h:nh + h + 1, :], 0.0)) for h in heads]
    kb = [k[h] * beta[h] for h in heads]
    gram = [_dot_nt(jnp.concatenate([kb[h].astype(BF16), q[h].astype(BF16)], axis=0), k[h].astype(BF16))
            for h in heads]
    t_inv = _inv_unit_lower_multi([jnp.where(strict, -gram[h][:c] * decay[h], 0.0) for h in heads], 6, refine=True)
    s0 = [s_ref[h] for h in heads]
    ws = [_dot(jnp.concatenate([(kb[h] * eg[h]).astype(BF16), (q[h] * eg[h]).astype(BF16)], axis=0), s0[h].astype(BF16))
          for h in heads]
    vnb = [_dot(t_inv[h].astype(BF16), (v_ref[rs, sls[h]] * beta[h] - ws[h][:c]).astype(BF16)).astype(BF16)
           for h in heads]
    for h in heads:
        attn = jnp.where(incl, gram[h][c:] * decay[h], 0.0)
        o = ws[h][c:] + _dot(attn.astype(BF16), vnb[h])
        o = o * lax.rsqrt(jnp.mean(o * o, axis=-1, keepdims=True) + NORM_EPS) * ng_ref[...]
        z = z_ref[rs, sls[h]].astype(F32)
        y_ref[rs, sls[h]] =(o * (z * _sigmoid(z))).astype(BF16)
    for h in heads:
        s_ref[h] = s0[h] * jnp.exp(glast[h]) + _dot_tn((k[h] * jnp.exp(glast[h] - gcol[h])).astype(BF16), vnb[h])


def _dn_mixer(pd, ps, batch, seq_len, conv_w, gparams, norm_g, tr=256):
    m = pd.shape[0]
    w = 1024
    row = lambda i: (i, 0)
    fix = lambda i: (0, 0)
    wide = jax.ShapeDtypeStruct((m, w), F32)
    q, k, v, gb = pl.pallas_call(
        functools.partial(_dn_pre_kernel, seq_len),
        out_shape=[wide, wide, wide, jax.ShapeDtypeStruct((m, LANES), F32)],
        grid=(m // tr,),
        in_specs=[
            pl.BlockSpec((tr, 3 * w), row),
            pl.BlockSpec((HALO_BF16, 3 * w), lambda i: (jnp.maximum(i * (tr // HALO_BF16) - 1, 0), 0)),
            pl.BlockSpec((tr, LANES), lambda i: (i, 0)),
            pl.BlockSpec((DN_CONV, 3 * w), fix),
            pl.BlockSpec((2, LANES), fix),
        ],
        out_specs=[pl.BlockSpec((tr, w), row)] * 3 + [pl.BlockSpec((tr, LANES), row)],
        compiler_params=_cparams(("parallel",)),
        name="deltanet_pre",
    )(pd, pd, ps, conv_w, gparams)

    c = DN_CHUNK * SCAN_CHUNKS_PER_STEP
    nchunk = seq_len // c
    blk = pl.BlockSpec((c, w), lambda bi, ci: (bi * nchunk + ci, 0))
    return pl.pallas_call(
        _dn_scan_kernel,
        out_shape=jax.ShapeDtypeStruct((m, w), BF16),
        grid=(batch, nchunk),
        in_specs=[blk, blk, blk,
                  pl.BlockSpec((c, w), lambda bi, ci: (bi * nchunk + ci, 3)),
                  pl.BlockSpec((c, LANES), lambda bi, ci: (bi * nchunk + ci, 0)),
                  pl.BlockSpec((1, HD), lambda bi, ci: (0, 0))],
        out_specs=blk,
        scratch_shapes=[pltpu.VMEM((w // HD, HD, HD), F32)],
        compiler_params=_cparams(("parallel", "arbitrary")),
        name="deltanet_scan",
    )(q, k, v, pd, gb, norm_g)


def _merge_kernel(hn_ref, ya_ref, yb_ref, yc_ref, yd_ref, wg_ref, bg_ref, wb_ref, o_ref, acc_ref):
    i = pl.program_id(1)

    @pl.when(i == 0)
    def _():
        acc_ref[...] = jnp.zeros_like(acc_ref)

    gate = _sigmoid(_dot(hn_ref[...], wg_ref[...]) + bg_ref[...])
    ys = (ya_ref, yb_ref, yc_ref, yd_ref)
    for n in range(4):
        @pl.when(i == n)
        def _(n=n):
            acc_ref[...] += gate * _dot(ys[n][...], wb_ref[...])

    @pl.when(i == 3)
    def _():
        o_ref[...] = acc_ref[...].astype(BF16)


def _outproj_kernel(x_ref, m_ref, w_ref, g_ref, o_ref):
    o_ref[...] = x_ref[...] + _rms(_dot(m_ref[...], w_ref[...]), g_ref[...])


def _merge(x, hn, g3, ys, w_gate, b_gate, w_branch, w_out, l, tm=512):
    m, d = x.shape
    bw = ys[0].shape[1]
    row = lambda r, i: (r, 0)
    merged = pl.pallas_call(
        _merge_kernel,
        out_shape=jax.ShapeDtypeStruct((m, d), BF16),
        grid=(m // tm, 4),
        in_specs=[
            pl.BlockSpec((tm, d), row),
            pl.BlockSpec((tm, bw), row),
            pl.BlockSpec((tm, bw), row),
            pl.BlockSpec((tm, bw), row),
            pl.BlockSpec((tm, bw), row),
            pl.BlockSpec((None, None, d, d), lambda r, i: (l, i, 0, 0)),
            pl.BlockSpec((None, None, 1, d), lambda r, i: (l, i, 0, 0)),
            pl.BlockSpec((None, None, bw, d), lambda r, i: (l, i, 0, 0)),
        ],
        out_specs=pl.BlockSpec((tm, d), row),
        scratch_shapes=[pltpu.VMEM((tm, d), F32)],
        compiler_params=_cparams(("parallel", "arbitrary")),
        name="merge",
    )(hn, ys[0], ys[1], ys[2], ys[3], w_gate, b_gate, w_branch)
    return pl.pallas_call(
        _outproj_kernel,
        out_shape=jax.ShapeDtypeStruct((m, d), F32),
        grid=(m // tm,),
        in_specs=[
            pl.BlockSpec((tm, d), lambda r: (r, 0)),
            pl.BlockSpec((tm, d), lambda r: (r, 0)),
            pl.BlockSpec((None, d, d), lambda r: (l, 0, 0)),
            pl.BlockSpec((1, d), lambda r: (0, 0)),
        ],
        out_specs=pl.BlockSpec((tm, d), lambda r: (r, 0)),
        compiler_params=_cparams(("parallel",)),
        name="outproj",
    )(x, merged, w_out, g3)


def _pad_cols(w, n):
    return jnp.pad(w, ((0, 0), (0, n - w.shape[1])))


def _pad_rows(w, n):
    return jnp.pad(w, ((0, n - w.shape[0]), (0, 0)))


def kernel(x, norm_g, ffn_w_in, ffn_w_out, w_in, w_gate, b_gate, w_branch, w_out, rwkv_mu, rwkv_vec, rwkv_r_k,
           rwkv_w2, rwkv_a2, rwkv_g2, fox_qk_g, fox_b_f, dn_conv, dn_A_log, dn_dt_bias, dn_norm_g):
    batch, seq_len, d = x.shape
    depth = norm_g.shape[0]
    bw = w_branch.shape[2]
    rw_lo = rwkv_w2.shape[1]
    ra_lo = rwkv_a2.shape[1]
    rg_lo = rwkv_g2.shape[1]
    nh = bw // HD
    assert bw == 1024 and rw_lo <= LANES and ra_lo <= LANES and rw_lo + ra_lo >= LANES and rg_lo == 2 * LANES
    assert seq_len % (DN_CHUNK * SCAN_CHUNKS_PER_STEP) == 0 and seq_len % ATT_BLK == 0 and (batch * seq_len) % 1024 == 0

    rw_cols = 3 * bw + rw_lo + ra_lo + rg_lo
    sb_cols = 3 * bw
    fx_cols = 3 * bw + nh
    o_a, o_b, o_c, o_d = 0, rw_cols, rw_cols + sb_cols, rw_cols + sb_cols + fx_cols

    head_of = jnp.arange(bw) // RW_HD
    e_red = (head_of[:, None] == jnp.arange(LANES)[None, :]).astype(BF16)
    e_exp = e_red.T

    w_in_b = w_in.astype(BF16)
    w_gate_b, w_branch_b, w_out_b = w_gate.astype(BF16), w_branch.astype(BF16), w_out.astype(BF16)

    xf = x.reshape(batch * seq_len, d)
    for l in range(depth):
        g = norm_g[l]
        mu_p = jnp.pad(rwkv_mu[l], (0, RW_PA - rw_cols))[None, :]
        w2p = _pad_rows(rwkv_w2[l], LANES)
        a2p = _pad_rows(rwkv_a2[l], LANES)
        b_f_pad = jnp.pad(fox_b_f[l], (0, LANES - nh))[None, :]
        gparams = jnp.stack([jnp.pad(dn_A_log[l], (nh, LANES - 2 * nh)), jnp.pad(dn_dt_bias[l], (nh, LANES - 2 * nh))])

        xf = _ffn(xf, g[0:1], g[1:2], ffn_w_in, ffn_w_out, l, 0)

        hn = _norm(xf, g[2:3])
        pa = _proj(hn, w_in_b, l, o_a, rw_cols, RW_PA // 4, F32, name="proj_rwkv")
        pb = _proj(hn, w_in_b, l, o_b, 3 * bw, 1024, BF16, name="proj_sb")
        pc = _proj(hn, w_in_b, l, o_c, 3 * bw, 1024, BF16, name="proj_fox")
        pd = _proj(hn, w_in_b, l, o_d, 4 * bw, 1024, BF16, name="proj_dn")
        psf = _proj(hn, w_in_b, l, o_c + 3 * bw, nh, LANES, F32, name="proj_fgate")
        psd = _proj(hn, w_in_b, l, o_d + 4 * bw, 2 * nh, LANES, F32, name="proj_dgate")

        ya = _rwkv_mixer(pa, batch, seq_len, rw_lo, ra_lo, mu_p, rwkv_vec[l], rwkv_r_k[l].reshape(1, bw), w2p, a2p,
                         rwkv_g2[l], e_red, e_exp)
        yb = _sb_mixer(pb, batch, seq_len)
        yc = _fox_mixer(pc, psf, batch, seq_len, fox_qk_g[l], b_f_pad)
        yd = _dn_mixer(pd, psd, batch, seq_len, dn_conv[l], gparams, dn_norm_g[l][None, :])

        xf = _merge(xf, hn, g[3:4], (ya, yb, yc, yd), w_gate_b, b_gate[:, :, None, :], w_branch_b, w_out_b, l)

        xf = _ffn(xf, g[4:5], g[5:6], ffn_w_in, ffn_w_out, l, 1)
    return xf.reshape(batch, seq_len, d)
```

```python
import functools

import jax
import jax.numpy as jnp
from jax import lax
from jax.experimental import pallas as pl
from jax.experimental.pallas import tpu as pltpu

F32 = jnp.float32
BF16 = jnp.bfloat16

NORM_EPS = 1e-6
RW_GN_EPS = 64e-5
L2_EPS = 1e-12
LOG2E = 1.4426950408889634

LANES = 128
SUBLANES = 8
VMEM_LIMIT = 58 * 1024 * 1024

RW_HD = 64
RW_CHUNK = 64
HD = 128
DN_CHUNK = 128
DN_CONV = 4
ATT_BLK = 256
ATT_HG = 4
FOX_HG = 8
SCAN_CHUNKS_PER_STEP = 2


def _cparams(sem):
    return pltpu.CompilerParams(dimension_semantics=sem, vmem_limit_bytes=VMEM_LIMIT)


def _dot(a, b):
    return jnp.dot(a, b, preferred_element_type=F32)


def _dot_nt(a, b):
    return lax.dot_general(a, b, (((1,), (1,)), ((), ())), preferred_element_type=F32)


def _dot_tn(a, b):
    return lax.dot_general(a, b, (((0,), (0,)), ((), ())), preferred_element_type=F32)


def _split(x):
    hi = x.astype(BF16)
    lo = (x - hi.astype(F32)).astype(BF16)
    return hi, lo


def _dot_x2(a_exact, x):
    hi, lo = _split(x)
    return _dot(a_exact, hi) + _dot(a_exact, lo)


def _dot_2x(x, b_exact):
    hi, lo = _split(x)
    return _dot(hi, b_exact) + _dot(lo, b_exact)


def _headsum(x, e_red, e_exp):
    return _dot_2x(_dot_2x(x, e_red), e_exp)


def _dot_3(a, b):
    ah, al = _split(a)
    bh, bl = _split(b)
    return _dot(ah, bh) + _dot(ah, bl) + _dot(al, bh)


def _sigmoid(x):
    return 1.0 / (1.0 + jnp.exp(-x))


def _softplus(x):
    return jnp.maximum(x, 0.0) + jnp.log(1.0 + jnp.exp(-jnp.abs(x)))


def _rms(x, g):
    return x * lax.rsqrt(jnp.mean(x * x, axis=-1, keepdims=True) + NORM_EPS) * g


def _iota2(shape, dim):
    return lax.broadcasted_iota(jnp.int32, shape, dim)


def _tri(n, strict):
    r, c = _iota2((n, n), 0), _iota2((n, n), 1)
    return (c < r) if strict else (c <= r)


def _shift_rows(x, halo, k):
    rolled = pltpu.roll(x, k, axis=0)
    hx = pltpu.roll(halo, k, axis=0)
    top = jnp.where(_iota2(hx.shape, 0) < k, hx, rolled[0:SUBLANES])
    return jnp.concatenate([top, rolled[SUBLANES:]], axis=0)


def _inv_unit_lower_multi(n_mats, steps, refine):
    n = n_mats[0].shape[0]
    eye = jnp.where(_iota2((n, n), 0) == _iota2((n, n), 1), 1.0, 0.0)
    ps = [eye + nm for nm in n_mats]
    ms = list(n_mats)
    for _ in range(steps):
        mbs = [m.astype(BF16) for m in ms]
        ms = [_dot(mb, mb) for mb in mbs]
        ps = [p + _dot(p.astype(BF16), m.astype(BF16)) for p, m in zip(ps, ms)]
    if not refine:
        return ps
    res = [(eye - p) + _dot_3(nm, p) for nm, p in zip(n_mats, ps)]
    return [p + _dot(p.astype(BF16), r.astype(BF16)) for p, r in zip(ps, res)]


def _ffn_kernel(x_ref, g0_ref, g1_ref, wg_ref, wu_ref, wo_ref, o_ref, xn_ref):
    j = pl.program_id(1)

    @pl.when(j == 0)
    def _():
        xn_ref[...] = _rms(x_ref[...], g0_ref[...]).astype(BF16)
        o_ref[...] = jnp.zeros_like(o_ref)

    xn = xn_ref[...]
    gate = _dot(xn, wg_ref[...].astype(BF16))
    up = _dot(xn, wu_ref[...].astype(BF16))
    act = (gate * _sigmoid(gate) * up).astype(BF16)
    o_ref[...] += _dot(act, wo_ref[...].astype(BF16))

    @pl.when(j == pl.num_programs(1) - 1)
    def _():
        o_ref[...] = x_ref[...] + 0.5 * _rms(o_ref[...], g1_ref[...])


def _ffn(x, g0, g1, w_in, w_out, l, k, tm=1024, tf=256):
    m, d = x.shape
    dff = w_out.shape[2]
    nf = dff // tf
    return pl.pallas_call(
        _ffn_kernel,
        out_shape=jax.ShapeDtypeStruct((m, d), F32),
        grid=(m // tm, nf),
        in_specs=[
            pl.BlockSpec((tm, d), lambda i, j: (i, 0)),
            pl.BlockSpec((1, d), lambda i, j: (0, 0)),
            pl.BlockSpec((1, d), lambda i, j: (0, 0)),
            pl.BlockSpec((None, None, d, tf), lambda i, j: (l, k, 0, j)),
            pl.BlockSpec((None, None, d, tf), lambda i, j: (l, k, 0, j + nf)),
            pl.BlockSpec((None, None, tf, d), lambda i, j: (l, k, j, 0)),
        ],
        out_specs=pl.BlockSpec((tm, d), lambda i, j: (i, 0)),
        scratch_shapes=[pltpu.VMEM((tm, d), BF16)],
        compiler_params=_cparams(("parallel", "arbitrary")),
        name="ffn",
    )(x, g0, g1, w_in, w_in, w_out)


def _norm_kernel(x_ref, g_ref, o_ref):
    o_ref[...] = _rms(x_ref[...], g_ref[...]).astype(BF16)


def _norm(x, g, tm=512):
    m, d = x.shape
    return pl.pallas_call(
        _norm_kernel,
        out_shape=jax.ShapeDtypeStruct((m, d), BF16),
        grid=(m // tm,),
        in_specs=[pl.BlockSpec((tm, d), lambda i: (i, 0)), pl.BlockSpec((1, d), lambda i: (0, 0))],
        out_specs=pl.BlockSpec((tm, d), lambda i: (i, 0)),
        compiler_params=_cparams(("parallel",)),
        name="norm",
    )(x, g)


def _proj_kernel(shift, width, two_blocks, h_ref, *refs):
    if two_blocks:
        w0_ref, w1_ref, o_ref, ws_ref = refs
    else:
        w0_ref, o_ref, ws_ref = refs
    j, i = pl.program_id(0), pl.program_id(1)
    tn = o_ref.shape[1]

    @pl.when(i == 0)
    def _():
        w = w0_ref[...]
        if two_blocks:
            w = jnp.concatenate([w, w1_ref[...]], axis=1)
        u = pltpu.bitcast(w, jnp.uint32)
        if shift:
            u = pltpu.roll(u, u.shape[1] - shift, axis=1)
        u = u[:, :tn]
        if width % tn:
            u = jnp.where(j * tn + _iota2(u.shape, 1) < width, u, jnp.uint32(0))
        ws_ref[...] = pltpu.bitcast(u, BF16)

    o_ref[...] = _dot(h_ref[...], ws_ref[...]).astype(o_ref.dtype)


def _proj(hn, w_all, l, off, width, tn, out_dtype, tm=1024, name="proj"):
    m, d = hn.shape
    c0, shift = off // tn, off % tn
    nt = -(-width // tn)
    two_blocks = shift > 0 and shift + min(width, tn) > tn
    w_specs = [pl.BlockSpec((None, d, tn), lambda j, i: (l, 0, c0 + j))]
    if two_blocks:
        w_specs.append(pl.BlockSpec((None, d, tn), lambda j, i: (l, 0, c0 + j + 1)))
    return pl.pallas_call(
        functools.partial(_proj_kernel, shift, width, two_blocks),
        out_shape=jax.ShapeDtypeStruct((m, nt * tn), out_dtype),
        grid=(nt, m // tm),
        in_specs=[pl.BlockSpec((tm, d), lambda j, i: (i, 0))] + w_specs,
        out_specs=pl.BlockSpec((tm, tn), lambda j, i: (i, j)),
        scratch_shapes=[pltpu.VMEM((d, tn), BF16)],
        compiler_params=_cparams(("parallel", "arbitrary")),
        name=name,
    )(hn, *([w_all] * len(w_specs)))


def _gate_proj_kernel(shifts, widths, h_ref, wa_ref, wb_ref, o_ref, ws_ref):
    @pl.when(pl.program_id(0) == 0)
    def _():
        for n, w_ref in enumerate((wa_ref, wb_ref)):
            u = pltpu.bitcast(w_ref[...], jnp.uint32)
            if shifts[n]:
                u = pltpu.roll(u, LANES - shifts[n], axis=1)
            u = jnp.where(_iota2(u.shape, 1) < widths[n], u, jnp.uint32(0))
            ws_ref[:, n * LANES:(n + 1) * LANES] = pltpu.bitcast(u, BF16)

    o_ref[...] = _dot(h_ref[...], ws_ref[...])


def _gate_proj(hn, w_all, l, offs, widths, tm=1024):
    m, d = hn.shape
    blocks = [o // LANES for o in offs]
    shifts = tuple(o % LANES for o in offs)
    assert len(offs) == 2 and all(s + w <= LANES for s, w in zip(shifts, widths))
    return pl.pallas_call(
        functools.partial(_gate_proj_kernel, shifts, tuple(widths)),
        out_shape=jax.ShapeDtypeStruct((m, 2 * LANES), F32),
        grid=(m // tm,),
        in_specs=[
            pl.BlockSpec((tm, d), lambda i: (i, 0)),
            pl.BlockSpec((None, d, LANES), lambda i: (l, 0, blocks[0])),
            pl.BlockSpec((None, d, LANES), lambda i: (l, 0, blocks[1])),
        ],
        out_specs=pl.BlockSpec((tm, 2 * LANES), lambda i: (i, 0)),
        scratch_shapes=[pltpu.VMEM((d, 2 * LANES), BF16)],
        compiler_params=_cparams(("arbitrary",)),
        name="proj_gates",
    )(hn, w_all, w_all)


RW_W = 1024
RW_PA = 3584


def _rwkv_pre_kernel(seq_len, rw_lo, ra_lo, p_ref, halo_ref, mu_ref, vec_ref, rk_ref, w2_ref, a2_ref, g2_ref,
                     er_ref, ex_ref, r_ref, lw_ref, k_ref, v_ref, nkk_ref, b_ref, g_ref, bonus_ref):
    tr = p_ref.shape[0]
    x = p_ref[...]
    first = (pl.program_id(0) * tr) % seq_len == 0
    halo = jnp.where(first, 0.0, halo_ref[...])
    prev = _shift_rows(x, halo, 1)
    x = x + (prev - x) * mu_ref[...]
    w = RW_W
    r, k, v = x[:, 0:w], x[:, w:2 * w], x[:, 2 * w:3 * w]
    tail = x[:, 3 * w:]
    lane = _iota2((tr, LANES), 1)
    w_lo = jnp.where(lane < rw_lo, tail[:, 0:LANES], 0.0)
    a_lo = jnp.where(lane < ra_lo, pltpu.roll(tail[:, 0:2 * LANES], 2 * LANES - rw_lo, axis=1)[:, 0:LANES], 0.0)
    g_off = rw_lo + ra_lo - LANES
    g_lo = pltpu.roll(tail[:, LANES:], 3 * LANES - g_off, axis=1)[:, 0:2 * LANES]
    w0, a0, k_k, k_a = vec_ref[0:1, :], vec_ref[1:2, :], vec_ref[2:3, :], vec_ref[3:4, :]
    lw = -jnp.exp(-0.5) * _sigmoid(w0 + _dot_3(jnp.tanh(w_lo), w2_ref[...]))
    a = _sigmoid(a0 + _dot_3(a_lo, a2_ref[...]))
    g = _dot_3(_sigmoid(g_lo), g2_ref[...])
    e_red, e_exp = er_ref[...], ex_ref[...]
    kk = k * k_k
    kk = kk * lax.rsqrt(_headsum(kk * kk, e_red, e_exp) + L2_EPS)
    k = k * (1.0 + (a - 1.0) * k_a)
    bonus = _headsum(r * k * rk_ref[...], e_red, e_exp) * v
    r_ref[...] = r
    lw_ref[...] = lw
    k_ref[...] = k
    v_ref[...] = v
    nkk_ref[...] = -kk
    b_ref[...] = kk * a
    g_ref[...] = g
    bonus_ref[...] = bonus


def _rwkv_scan_kernel(r_ref, lw_ref, k_ref, v_ref, nkk_ref, b_ref, o_ref, s_ref):
    @pl.when(pl.program_id(1) == 0)
    def _():
        s_ref[...] = jnp.zeros_like(s_ref)

    for cc in range(r_ref.shape[0] // RW_CHUNK):
        _rwkv_chunk(slice(cc * RW_CHUNK, (cc + 1) * RW_CHUNK), r_ref, lw_ref, k_ref, v_ref, nkk_ref, b_ref, o_ref, s_ref)


def _rwkv_chunk(rs, r_ref, lw_ref, k_ref, v_ref, nkk_ref, b_ref, o_ref, s_ref):
    c = RW_CHUNK
    lw = lw_ref[rs, :]
    ltri = jnp.where(_tri(c, strict=False), 1.0, 0.0).astype(BF16)
    cum = _dot_x2(ltri, lw)
    tot = cum[c - 1:c, :]
    e_in, e_neg = jnp.exp(cum), jnp.exp(-cum)
    e_ex, e_end = jnp.exp(cum - lw), jnp.exp(tot - cum)
    r, k, v, nkk, b = r_ref[rs, :], k_ref[rs, :], v_ref[rs, :], nkk_ref[rs, :], b_ref[rs, :]
    rt, kt, bt, at = r * e_in, k * e_neg, b * e_neg, nkk * e_ex
    kh, bh = k * e_end, b * e_end
    e_tot = jnp.exp(tot)

    n2 = 2 * c
    lane_lo = _iota2((c, LANES), 1) < RW_HD
    ri, ci = _iota2((n2, n2), 0) & (c - 1), _iota2((n2, n2), 1) & (c - 1)
    strict, incl = ci < ri, ci <= ri

    def stack(xp):
        return jnp.concatenate([jnp.where(lane_lo, xp, 0.0), jnp.where(lane_lo, 0.0, xp)], axis=0).astype(BF16)

    pairs = range(RW_W // LANES)
    sls = [slice(pr * LANES, (pr + 1) * LANES) for pr in pairs]
    at_s = [stack(at[:, sl]) for sl in sls]
    rt_s = [stack(rt[:, sl]) for sl in sls]
    bt_s = [stack(bt[:, sl]) for sl in sls]
    kt_s = [stack(kt[:, sl]) for sl in sls]
    v_s = [stack(v[:, sl]) for sl in sls]
    gram = [_dot_nt(jnp.concatenate([at_s[p], rt_s[p]], axis=0), jnp.concatenate([bt_s[p], kt_s[p]], axis=0))
            for p in pairs]
    t_inv = _inv_unit_lower_multi([jnp.where(strict, gm[:n2, :n2], 0.0) for gm in gram], 5, refine=False)
    akv = [_dot(jnp.where(strict, gram[p][:n2, n2:], 0.0).astype(BF16), v_s[p]) for p in pairs]
    s0 = [s_ref[p] for p in pairs]
    ws = [_dot_nt(jnp.concatenate([at_s[p], rt_s[p]], axis=0), s0[p].astype(BF16)) for p in pairs]
    u = [_dot(t_inv[p].astype(BF16), (ws[p][:n2] + akv[p]).astype(BF16)) for p in pairs]
    uv = [jnp.concatenate([u[p].astype(BF16), v_s[p]], axis=0) for p in pairs]
    for p in pairs:
        a_r = jnp.concatenate([jnp.where(incl, gram[p][n2:, :n2], 0.0), jnp.where(incl, gram[p][n2:, n2:], 0.0)], axis=1)
        o_s = ws[p][n2:] + _dot(a_r.astype(BF16), uv[p])
        o_ref[rs, sls[p]] = o_s[:c] + o_s[c:]
    for p in pairs:
        bk = jnp.concatenate([stack(bh[:, sls[p]]), stack(kh[:, sls[p]])], axis=0)
        s_ref[p] = s0[p] * e_tot[:, sls[p]] + _dot_tn(uv[p], bk)


def _rwkv_post_kernel(o_ref, bonus_ref, g_ref, ln_ref, er_ref, ex_ref, y_ref):
    o = o_ref[...]
    e_red, e_exp = er_ref[...], ex_ref[...]
    mean = _headsum(o, e_red, e_exp) * (1.0 / RW_HD)
    d = o - mean
    var = _headsum(d * d, e_red, e_exp) * (1.0 / RW_HD)
    o = d * lax.rsqrt(var + RW_GN_EPS) * ln_ref[0:1, :] + ln_ref[1:2, :]
    y_ref[...] = ((o + bonus_ref[...]) * g_ref[...]).astype(BF16)


def _rwkv_mixer(pa, batch, seq_len, rw_lo, ra_lo, mu, vec, r_k, w2, a2, g2, e_red, e_exp, tr=256):
    m = pa.shape[0]
    w = RW_W
    row = lambda i: (i, 0)
    fix = lambda i: (0, 0)
    wide = jax.ShapeDtypeStruct((m, w), F32)
    r, lw, k, v, nkk, b, g, bonus = pl.pallas_call(
        functools.partial(_rwkv_pre_kernel, seq_len, rw_lo, ra_lo),
        out_shape=[wide] * 8,
        grid=(m // tr,),
        in_specs=[
            pl.BlockSpec((tr, RW_PA), row),
            pl.BlockSpec((SUBLANES, RW_PA), lambda i: (jnp.maximum(i * (tr // SUBLANES) - 1, 0), 0)),
            pl.BlockSpec((1, RW_PA), fix),
            pl.BlockSpec((4, w), fix),
            pl.BlockSpec((1, w), fix),
            pl.BlockSpec((LANES, w), fix),
            pl.BlockSpec((LANES, w), fix),
            pl.BlockSpec((2 * LANES, w), fix),
            pl.BlockSpec((w, LANES), fix),
            pl.BlockSpec((LANES, w), fix),
        ],
        out_specs=[pl.BlockSpec((tr, w), row)] * 8,
        compiler_params=_cparams(("parallel",)),
        name="rwkv_pre",
    )(pa, pa, mu, vec[0:4], r_k, w2, a2, g2, e_red, e_exp)

    c = RW_CHUNK * SCAN_CHUNKS_PER_STEP
    nchunk = seq_len // c
    blk = pl.BlockSpec((c, w), lambda bi, ci: (bi * nchunk + ci, 0))
    o = pl.pallas_call(
        _rwkv_scan_kernel,
        out_shape=wide,
        grid=(batch, nchunk),
        in_specs=[blk] * 6,
        out_specs=blk,
        scratch_shapes=[pltpu.VMEM((w // LANES, LANES, LANES), F32)],
        compiler_params=_cparams(("parallel", "arbitrary")),
        name="rwkv_scan",
    )(r, lw, k, v, nkk, b)

    return pl.pallas_call(
        _rwkv_post_kernel,
        out_shape=jax.ShapeDtypeStruct((m, w), BF16),
        grid=(m // tr,),
        in_specs=[pl.BlockSpec((tr, w), row)] * 3 + [pl.BlockSpec((2, w), fix), pl.BlockSpec((w, LANES), fix),
                  pl.BlockSpec((LANES, w), fix)],
        out_specs=pl.BlockSpec((tr, w), row),
        compiler_params=_cparams(("parallel",)),
        name="rwkv_post",
    )(o, bonus, g, vec[4:6], e_red, e_exp)


def _sb_kernel(q_ref, kb_ref, vb_ref, o_ref):
    i = pl.program_id(2)
    tb = ATT_BLK
    heads = range(ATT_HG)
    hsl = [slice(hh * HD, (hh + 1) * HD) for hh in heads]
    q = [(q_ref[:, sl].astype(F32) * (HD ** -0.5 * LOG2E)).astype(BF16) for sl in hsl]
    r, c = _iota2((tb, tb), 0), _iota2((tb, tb), 1)
    upper = jnp.where(r > c, 1.0, 0.0).astype(BF16)
    diag_ok = c < r

    def block(j, carry, acc, masked):
        rows = pl.ds(pl.multiple_of(j * tb, tb), tb)
        z = [_dot_nt(q[hh], kb_ref[rows, hsl[hh]]) for hh in heads]
        nz = [-zz for zz in z]
        l1 = [jnp.log2(1.0 + jnp.exp2(jnp.minimum(z[hh], nz[hh]))) for hh in heads]
        lk = [jnp.minimum(nz[hh], 0.0) - l1[hh] for hh in heads]
        if masked:
            lk = [jnp.where(diag_ok, x, 0.0) for x in lk]
        later = [_dot(lk[hh].astype(BF16), upper) + carry[hh] for hh in heads]
        a = [jnp.exp2((jnp.minimum(z[hh], 0.0) - l1[hh]) + later[hh]) for hh in heads]
        if masked:
            a = [jnp.where(diag_ok, aa, 0.0) for aa in a]
        acc = [acc[hh] + _dot(a[hh].astype(BF16), vb_ref[rows, hsl[hh]]) for hh in heads]
        carry = [later[hh][:, 0:1] + lk[hh][:, 0:1] for hh in heads]
        return carry, acc

    st = block(i, [jnp.zeros((tb, 1), F32)] * ATT_HG, [jnp.zeros((tb, HD), F32)] * ATT_HG, True)

    def body(n, st):
        return block(i - 1 - n, st[0], st[1], False)

    carry, acc = lax.fori_loop(0, i, body, st)
    for hh in heads:
        o_ref[:, hsl[hh]] = acc[hh].astype(BF16)


def _sb_mixer(pb, batch, seq_len):
    m = pb.shape[0]
    tb = ATT_BLK
    nq = seq_len // tb
    ng = 1024 // (HD * ATT_HG)
    gw = HD * ATT_HG
    return pl.pallas_call(
        _sb_kernel,
        out_shape=jax.ShapeDtypeStruct((m, 1024), BF16),
        grid=(batch, ng, nq),
        in_specs=[
            pl.BlockSpec((tb, gw), lambda b, h, i: (b * nq + i, h)),
            pl.BlockSpec((seq_len, gw), lambda b, h, i: (b, ng + h)),
            pl.BlockSpec((seq_len, gw), lambda b, h, i: (b, 2 * ng + h)),
        ],
        out_specs=pl.BlockSpec((tb, gw), lambda b, h, i: (b * nq + i, h)),
        compiler_params=_cparams(("parallel", "parallel", "arbitrary")),
        name="stick_breaking",
    )(pb, pb, pb)


def _fox_cum_kernel(f_ref, bf_ref, cum_ref, cumt_ref):
    tb = LANES
    x = f_ref[...] + bf_ref[...]
    lf = jnp.minimum(x, 0.0) - jnp.log(1.0 + jnp.exp(-jnp.abs(x)))
    ltri = jnp.where(_tri(tb, strict=False), 1.0, 0.0).astype(BF16)
    carry = jnp.zeros((1, LANES), F32)
    for blk in range(f_ref.shape[0] // tb):
        xb = lf[blk * tb:(blk + 1) * tb, :]
        hi = xb.astype(BF16)
        mid = (xb - hi.astype(F32)).astype(BF16)
        lo = (xb - hi.astype(F32) - mid.astype(F32)).astype(BF16)
        cb = _dot(ltri, hi) + _dot(ltri, mid) + _dot(ltri, lo) + carry
        cum_ref[blk * tb:(blk + 1) * tb, :] = cb
        cumt_ref[:, blk * tb:(blk + 1) * tb] = cb.T
        carry = cb[tb - 1:tb, :]


def _fox_kernel(q_ref, k_ref, vb_ref, g_ref, cq_ref, ck_ref, o_ref, kn_ref):
    hg = pl.program_id(1)
    i = pl.program_id(2)
    tb = ATT_BLK
    heads = range(FOX_HG)
    hsl = [slice(hh * HD, (hh + 1) * HD) for hh in heads]

    def hnorm(x, g):
        x = x.astype(F32)
        return x * lax.rsqrt(jnp.mean(x * x, axis=-1, keepdims=True) + NORM_EPS) * g

    @pl.when(i == 0)
    def _():
        for sl in hsl:
            kn_ref[:, sl] = hnorm(k_ref[:, sl], g_ref[1:2, :]).astype(BF16)

    q = [(hnorm(q_ref[:, sl], g_ref[0:1, :]) * (HD ** -0.5 * LOG2E)).astype(BF16) for sl in hsl]
    cq_all = cq_ref[...] * LOG2E
    lane = _iota2((tb, LANES), 1)
    cq = [jnp.sum(jnp.where(lane == hg * FOX_HG + hh, cq_all, 0.0), axis=-1, keepdims=True) for hh in heads]
    r, c = _iota2((tb, tb), 0), _iota2((tb, tb), 1)
    causal = c <= r

    def block(j, m_run, l_run, acc, masked):
        rows = pl.ds(pl.multiple_of(j * tb, tb), tb)
        ck8 = ck_ref[:, rows]
        sub = _iota2(ck8.shape, 0)
        ck = [jnp.sum(jnp.where(sub == hg * FOX_HG + hh, ck8, 0.0), axis=0, keepdims=True) * LOG2E for hh in heads]
        s = [_dot_nt(q[hh], kn_ref[rows, hsl[hh]]) - ck[hh] for hh in heads]
        if masked:
            s = [jnp.where(causal, ss, -jnp.inf) for ss in s]
        m_new = [jnp.maximum(m_run[hh], jnp.max(s[hh], axis=-1, keepdims=True) + cq[hh]) for hh in heads]
        alpha = [jnp.exp2(m_run[hh] - m_new[hh]) for hh in heads]
        p = [jnp.exp2(s[hh] - (m_new[hh] - cq[hh])) for hh in heads]
        l_new = [alpha[hh] * l_run[hh] + jnp.sum(p[hh], axis=-1, keepdims=True) for hh in heads]
        acc = [alpha[hh] * acc[hh] + _dot(p[hh].astype(BF16), vb_ref[rows, hsl[hh]]) for hh in heads]
        return m_new, l_new, acc

    st = block(i, [jnp.full((tb, 1), -jnp.inf, F32)] * FOX_HG, [jnp.zeros((tb, 1), F32)] * FOX_HG,
               [jnp.zeros((tb, HD), F32)] * FOX_HG, True)

    def body(n, st):
        return block(i - 1 - n, st[0], st[1], st[2], False)

    m_run, l_run, acc = lax.fori_loop(0, i, body, st)
    for hh in heads:
        o_ref[:, hsl[hh]] = (acc[hh] / l_run[hh]).astype(BF16)


def _fox_mixer(pc, ps, batch, seq_len, qk_g, b_f_pad):
    m = pc.shape[0]
    tb = ATT_BLK
    nq = seq_len // tb
    cum, cumt = pl.pallas_call(
        _fox_cum_kernel,
        out_shape=[jax.ShapeDtypeStruct((m, LANES), F32), jax.ShapeDtypeStruct((batch * LANES, seq_len), F32)],
        grid=(batch,),
        in_specs=[pl.BlockSpec((seq_len, LANES), lambda b: (b, 0)), pl.BlockSpec((1, LANES), lambda b: (0, 0))],
        out_specs=[pl.BlockSpec((seq_len, LANES), lambda b: (b, 0)), pl.BlockSpec((LANES, seq_len), lambda b: (b, 0))],
        compiler_params=_cparams(("parallel",)),
        name="fox_cum",
    )(ps, b_f_pad)
    ng = 1024 // (HD * FOX_HG)
    gw = HD * FOX_HG
    return pl.pallas_call(
        _fox_kernel,
        out_shape=jax.ShapeDtypeStruct((m, 1024), BF16),
        grid=(batch, ng, nq),
        in_specs=[
            pl.BlockSpec((tb, gw), lambda b, h, i: (b * nq + i, h)),
            pl.BlockSpec((seq_len, gw), lambda b, h, i: (b, ng + h)),
            pl.BlockSpec((seq_len, gw), lambda b, h, i: (b, 2 * ng + h)),
            pl.BlockSpec((2, HD), lambda b, h, i: (0, 0)),
            pl.BlockSpec((tb, LANES), lambda b, h, i: (b * nq + i, 0)),
            pl.BlockSpec((SUBLANES, seq_len), lambda b, h, i: (b * (LANES // SUBLANES), 0)),
        ],
        out_specs=pl.BlockSpec((tb, gw), lambda b, h, i: (b * nq + i, h)),
        scratch_shapes=[pltpu.VMEM((seq_len, gw), BF16)],
        compiler_params=_cparams(("parallel", "parallel", "arbitrary")),
        name="forgetting_attention",
    )(pc, pc, pc, qk_g, cum, cumt)


HALO_BF16 = 16


def _dn_pre_kernel(seq_len, p_ref, halo_ref, s_ref, cw_ref, gp_ref, q_ref, k_ref, v_ref, gb_ref):
    tr = p_ref.shape[0]
    w = 1024
    first = (pl.program_id(0) * tr) % seq_len == 0
    x = p_ref[...].astype(F32)
    halo = jnp.where(first, 0.0, halo_ref[HALO_BF16 - SUBLANES:, :].astype(F32))
    acc = x * cw_ref[DN_CONV - 1:DN_CONV, :]
    for s in range(1, DN_CONV):
        acc = acc + _shift_rows(x, halo, s) * cw_ref[DN_CONV - 1 - s:DN_CONV - s, :]
    y = acc * _sigmoid(acc)
    for hh in range(w // HD):
        qs = y[:, hh * HD:(hh + 1) * HD]
        ks = y[:, w + hh * HD:w + (hh + 1) * HD]
        q_ref[:, hh * HD:(hh + 1) * HD] = qs * lax.rsqrt(jnp.sum(qs * qs, -1, keepdims=True) + L2_EPS) * (HD ** -0.5)
        k_ref[:, hh * HD:(hh + 1) * HD] = ks * lax.rsqrt(jnp.sum(ks * ks, -1, keepdims=True) + L2_EPS)
    v_ref[...] = y[:, 2 * w:3 * w]
    s = s_ref[...]
    nh = w // HD
    beta = _sigmoid(s)
    g = -jnp.exp(gp_ref[0:1, :]) * _softplus(s + gp_ref[1:2, :])
    gb_ref[...] = jnp.where(_iota2(s.shape, 1) < nh, beta, g)


def _dn_scan_kernel(q_ref, k_ref, v_ref, z_ref, gb_ref, ng_ref, y_ref, s_ref):
    @pl.when(pl.program_id(1) == 0)
    def _():
        s_ref[...] = jnp.zeros_like(s_ref)

    for cc in range(q_ref.shape[0] // DN_CHUNK):
        _dn_chunk(slice(cc * DN_CHUNK, (cc + 1) * DN_CHUNK), q_ref, k_ref, v_ref, z_ref, gb_ref, ng_ref, y_ref, s_ref)


def _dn_chunk(rs, q_ref, k_ref, v_ref, z_ref, gb_ref, ng_ref, y_ref, s_ref):
    c = DN_CHUNK
    nh = 1024 // HD
    gb = gb_ref[rs, :]
    ltri = jnp.where(_tri(c, strict=False), 1.0, 0.0).astype(BF16)
    hi = gb.astype(BF16)
    mid = (gb - hi.astype(F32)).astype(BF16)
    lo = (gb - hi.astype(F32) - mid.astype(F32)).astype(BF16)
    gc = _dot(ltri, hi) + _dot(ltri, mid) + _dot(ltri, lo)
    gct = gc.T
    strict, incl = _tri(c, strict=True), _tri(c, strict=False)
    heads = range(nh)
    sls = [slice(h * HD, (h + 1) * HD) for h in heads]
    q = [q_ref[rs, sl] for sl in sls]
    k = [k_ref[rs, sl] for sl in sls]
    beta = [gb[:, h:h + 1] for h in heads]
    gcol = [gc[:, nh + h:nh + h + 1] for h in heads]
    glast = [gc[c - 1:c, nh + h:nh + h + 1] for h in heads]
    eg = [jnp.exp(gcol[h]) for h in heads]
    decay = [jnp.exp(jnp.minimum(gcol[h] - gct[nh + h:nh + h + 1, :], 0.0)) for h in heads]
    kb = [k[h] * beta[h] for h in heads]
    gram = [_dot_nt(jnp.concatenate([kb[h].astype(BF16), q[h].astype(BF16)], axis=0), k[h].astype(BF16))
            for h in heads]
    t_inv = _inv_unit_lower_multi([jnp.where(strict, -gram[h][:c] * decay[h], 0.0) for h in heads], 6, refine=True)
    s0 = [s_ref[h] for h in heads]
    ws = [_dot(jnp.concatenate([(kb[h] * eg[h]).astype(BF16), (q[h] * eg[h]).astype(BF16)], axis=0), s0[h].astype(BF16))
          for h in heads]
    vnb = [_dot(t_inv[h].astype(BF16), (v_ref[rs, sls[h]] * beta[h] - ws[h][:c]).astype(BF16)).astype(BF16)
           for h in heads]
    for h in heads:
        attn = jnp.where(incl, gram[h][c:] * decay[h], 0.0)
        o = ws[h][c:] + _dot(attn.astype(BF16), vnb[h])
        o = o * lax.rsqrt(jnp.mean(o * o, axis=-1, keepdims=True) + NORM_EPS) * ng_ref[...]
        z = z_ref[rs, sls[h]].astype(F32)
        y_ref[rs, sls[h]] =(o * (z * _sigmoid(z))).astype(BF16)
    for h in heads:
        s_ref[h] = s0[h] * jnp.exp(glast[h]) + _dot_tn((k[h] * jnp.exp(glast[h] - gcol[h])).astype(BF16), vnb[h])


def _dn_mixer(pd, ps, batch, seq_len, conv_w, gparams, norm_g, tr=256):
    m = pd.shape[0]
    w = 1024
    row = lambda i: (i, 0)
    fix = lambda i: (0, 0)
    wide = jax.ShapeDtypeStruct((m, w), F32)
    q, k, v, gb = pl.pallas_call(
        functools.partial(_dn_pre_kernel, seq_len),
        out_shape=[wide, wide, wide, jax.ShapeDtypeStruct((m, LANES), F32)],
        grid=(m // tr,),
        in_specs=[
            pl.BlockSpec((tr, 3 * w), row),
            pl.BlockSpec((HALO_BF16, 3 * w), lambda i: (jnp.maximum(i * (tr // HALO_BF16) - 1, 0), 0)),
            pl.BlockSpec((tr, LANES), lambda i: (i, 1)),
            pl.BlockSpec((DN_CONV, 3 * w), fix),
            pl.BlockSpec((2, LANES), fix),
        ],
        out_specs=[pl.BlockSpec((tr, w), row)] * 3 + [pl.BlockSpec((tr, LANES), row)],
        compiler_params=_cparams(("parallel",)),
        name="deltanet_pre",
    )(pd, pd, ps, conv_w, gparams)

    c = DN_CHUNK * SCAN_CHUNKS_PER_STEP
    nchunk = seq_len // c
    blk = pl.BlockSpec((c, w), lambda bi, ci: (bi * nchunk + ci, 0))
    return pl.pallas_call(
        _dn_scan_kernel,
        out_shape=jax.ShapeDtypeStruct((m, w), BF16),
        grid=(batch, nchunk),
        in_specs=[blk, blk, blk,
                  pl.BlockSpec((c, w), lambda bi, ci: (bi * nchunk + ci, 3)),
                  pl.BlockSpec((c, LANES), lambda bi, ci: (bi * nchunk + ci, 0)),
                  pl.BlockSpec((1, HD), lambda bi, ci: (0, 0))],
        out_specs=blk,
        scratch_shapes=[pltpu.VMEM((w // HD, HD, HD), F32)],
        compiler_params=_cparams(("parallel", "arbitrary")),
        name="deltanet_scan",
    )(q, k, v, pd, gb, norm_g)


def _merge_kernel(hn_ref, ya_ref, yb_ref, yc_ref, yd_ref, wg_ref, bg_ref, wb_ref, o_ref, acc_ref):
    i = pl.program_id(1)

    @pl.when(i == 0)
    def _():
        acc_ref[...] = jnp.zeros_like(acc_ref)

    gate = _sigmoid(_dot(hn_ref[...], wg_ref[...]) + bg_ref[...])
    ys = (ya_ref, yb_ref, yc_ref, yd_ref)
    for n in range(4):
        @pl.when(i == n)
        def _(n=n):
            acc_ref[...] += gate * _dot(ys[n][...], wb_ref[...])

    @pl.when(i == 3)
    def _():
        o_ref[...] = acc_ref[...].astype(BF16)


def _outproj_kernel(x_ref, m_ref, w_ref, g_ref, o_ref):
    o_ref[...] = x_ref[...] + _rms(_dot(m_ref[...], w_ref[...]), g_ref[...])


def _merge(x, hn, g3, ys, w_gate, b_gate, w_branch, w_out, l, tm=512):
    m, d = x.shape
    bw = ys[0].shape[1]
    row = lambda r, i: (r, 0)
    merged = pl.pallas_call(
        _merge_kernel,
        out_shape=jax.ShapeDtypeStruct((m, d), BF16),
        grid=(m // tm, 4),
        in_specs=[
            pl.BlockSpec((tm, d), row),
            pl.BlockSpec((tm, bw), row),
            pl.BlockSpec((tm, bw), row),
            pl.BlockSpec((tm, bw), row),
            pl.BlockSpec((tm, bw), row),
            pl.BlockSpec((None, None, d, d), lambda r, i: (l, i, 0, 0)),
            pl.BlockSpec((None, None, 1, d), lambda r, i: (l, i, 0, 0)),
            pl.BlockSpec((None, None, bw, d), lambda r, i: (l, i, 0, 0)),
        ],
        out_specs=pl.BlockSpec((tm, d), row),
        scratch_shapes=[pltpu.VMEM((tm, d), F32)],
        compiler_params=_cparams(("parallel", "arbitrary")),
        name="merge",
    )(hn, ys[0], ys[1], ys[2], ys[3], w_gate, b_gate, w_branch)
    return pl.pallas_call(
        _outproj_kernel,
        out_shape=jax.ShapeDtypeStruct((m, d), F32),
        grid=(m // tm,),
        in_specs=[
            pl.BlockSpec((tm, d), lambda r: (r, 0)),
            pl.BlockSpec((tm, d), lambda r: (r, 0)),
            pl.BlockSpec((None, d, d), lambda r: (l, 0, 0)),
            pl.BlockSpec((1, d), lambda r: (0, 0)),
        ],
        out_specs=pl.BlockSpec((tm, d), lambda r: (r, 0)),
        compiler_params=_cparams(("parallel",)),
        name="outproj",
    )(x, merged, w_out, g3)


def _pad_cols(w, n):
    return jnp.pad(w, ((0, 0), (0, n - w.shape[1])))


def _pad_rows(w, n):
    return jnp.pad(w, ((0, n - w.shape[0]), (0, 0)))


def kernel(x, norm_g, ffn_w_in, ffn_w_out, w_in, w_gate, b_gate, w_branch, w_out, rwkv_mu, rwkv_vec, rwkv_r_k,
           rwkv_w2, rwkv_a2, rwkv_g2, fox_qk_g, fox_b_f, dn_conv, dn_A_log, dn_dt_bias, dn_norm_g):
    batch, seq_len, d = x.shape
    depth = norm_g.shape[0]
    bw = w_branch.shape[2]
    rw_lo = rwkv_w2.shape[1]
    ra_lo = rwkv_a2.shape[1]
    rg_lo = rwkv_g2.shape[1]
    nh = bw // HD
    assert bw == 1024 and rw_lo <= LANES and ra_lo <= LANES and rw_lo + ra_lo >= LANES and rg_lo == 2 * LANES
    assert seq_len % (DN_CHUNK * SCAN_CHUNKS_PER_STEP) == 0 and seq_len % ATT_BLK == 0 and (batch * seq_len) % 1024 == 0

    rw_cols = 3 * bw + rw_lo + ra_lo + rg_lo
    sb_cols = 3 * bw
    fx_cols = 3 * bw + nh
    o_a, o_b, o_c, o_d = 0, rw_cols, rw_cols + sb_cols, rw_cols + sb_cols + fx_cols

    head_of = jnp.arange(bw) // RW_HD
    e_red = (head_of[:, None] == jnp.arange(LANES)[None, :]).astype(BF16)
    e_exp = e_red.T

    w_in_b = w_in.astype(BF16)
    w_gate_b, w_branch_b, w_out_b = w_gate.astype(BF16), w_branch.astype(BF16), w_out.astype(BF16)

    xf = x.reshape(batch * seq_len, d)
    for l in range(depth):
        g = norm_g[l]
        mu_p = jnp.pad(rwkv_mu[l], (0, RW_PA - rw_cols))[None, :]
        w2p = _pad_rows(rwkv_w2[l], LANES)
        a2p = _pad_rows(rwkv_a2[l], LANES)
        b_f_pad = jnp.pad(fox_b_f[l], (0, LANES - nh))[None, :]
        gparams = jnp.stack([jnp.pad(dn_A_log[l], (nh, LANES - 2 * nh)), jnp.pad(dn_dt_bias[l], (nh, LANES - 2 * nh))])

        xf = _ffn(xf, g[0:1], g[1:2], ffn_w_in, ffn_w_out, l, 0)

        hn = _norm(xf, g[2:3])
        pa = _proj(hn, w_in_b, l, o_a, rw_cols, RW_PA // 4, F32, name="proj_rwkv")
        pb = _proj(hn, w_in_b, l, o_b, 3 * bw, 1024, BF16, name="proj_sb")
        pc = _proj(hn, w_in_b, l, o_c, 3 * bw, 1024, BF16, name="proj_fox")
        pd = _proj(hn, w_in_b, l, o_d, 4 * bw, 1024, BF16, name="proj_dn")
        ps = _gate_proj(hn, w_in_b, l, (o_c + 3 * bw, o_d + 4 * bw), (nh, 2 * nh))

        ya = _rwkv_mixer(pa, batch, seq_len, rw_lo, ra_lo, mu_p, rwkv_vec[l], rwkv_r_k[l].reshape(1, bw), w2p, a2p,
                         rwkv_g2[l], e_red, e_exp)
        yb = _sb_mixer(pb, batch, seq_len)
        yc = _fox_mixer(pc, ps, batch, seq_len, fox_qk_g[l], b_f_pad)
        yd = _dn_mixer(pd, ps, batch, seq_len, dn_conv[l], gparams, dn_norm_g[l][None, :])

        xf = _merge(xf, hn, g[3:4], (ya, yb, yc, yd), w_gate_b, b_gate[:, :, None, :], w_branch_b, w_out_b, l)

        xf = _ffn(xf, g[4:5], g[5:6], ffn_w_in, ffn_w_out, l, 1)
    return xf.reshape(batch, seq_len, d)
```
